```python
import jax, jax.numpy as jnp
from jax import lax
import numpy as np

D_MODEL = 1024
BATCH = 2
SEQ = 16384
DEPTH = 1

PLE_DIM = 256
CONV_WIDTH = 1024
CONV_K = 3
N_HEADS = 4
QK_DIM = 1024
V_DIM = 2048
DK = QK_DIM // N_HEADS
DV = V_DIM // N_HEADS
CHUNK = 64
EPS = 1e-6
SPLIT_SIZES = (CONV_WIDTH, CONV_WIDTH, CONV_WIDTH, CONV_WIDTH,
               QK_DIM, QK_DIM, V_DIM, V_DIM, V_DIM, N_HEADS, N_HEADS,
               D_MODEL, D_MODEL)
N_IN = sum(SPLIT_SIZES)
SPLIT_POINTS = tuple(int(s) for s in np.cumsum(SPLIT_SIZES)[:-1])

kernel_name = "hybrid_shortconv_mlstm_gated_merge"


def rmsnorm(x, g):
    xf = x.astype(jnp.float32)
    y = xf * lax.rsqrt(jnp.mean(xf * xf, axis=-1, keepdims=True) + EPS)
    return (y * g.astype(jnp.float32)).astype(x.dtype)


def causal_short_conv(u, w, b):
    s = u.shape[1]
    up = jnp.pad(u, ((0, 0), (CONV_K - 1, 0), (0, 0)))
    y = up[:, 0:s] * w[0]
    for j in range(1, CONV_K):
        y = y + up[:, j:j + s] * w[j]
    return y + b


def mlstm_chunkwise(q, k, v, i_raw, f_raw):
    bsz, s = q.shape[0], q.shape[1]
    nc = s // CHUNK
    f32 = jnp.float32

    def to_chunks(t):
        return t.astype(f32).reshape(bsz, nc, CHUNK, N_HEADS, -1).transpose(1, 0, 3, 2, 4)

    def gate_chunks(t):
        return t.astype(f32).reshape(bsz, nc, CHUNK, N_HEADS).transpose(1, 0, 3, 2)

    qc = to_chunks(q) * (DK ** -0.5)
    kc = to_chunks(k)
    vc = to_chunks(v)
    lic = gate_chunks(i_raw)
    lfc = jax.nn.log_sigmoid(gate_chunks(f_raw))
    mask = jnp.tril(jnp.ones((CHUNK, CHUNK), dtype=bool))

    def step(carry, xs):
        c_st, n_st, m_st = carry
        qq, kk, vv, li, lf = xs
        bcum = jnp.cumsum(lf, axis=-1)
        dmat = bcum[..., :, None] - bcum[..., None, :] + li[..., None, :]
        dmat = jnp.where(mask, dmat, -jnp.inf)
        a = bcum + m_st[..., None]
        m_row = jnp.maximum(a, jnp.max(dmat, axis=-1))
        sc = jnp.einsum('bhld,bhsd->bhls', qq, kk) * jnp.exp(dmat - m_row[..., None])
        inter = jnp.exp(a - m_row)
        num = (jnp.einsum('bhls,bhsv->bhlv', sc, vv)
               + inter[..., None] * jnp.einsum('bhld,bhdv->bhlv', qq, c_st))
        den = jnp.sum(sc, axis=-1) + inter * jnp.einsum('bhld,bhd->bhl', qq, n_st)
        h = num / jnp.maximum(jnp.abs(den), jnp.exp(-m_row))[..., None]
        b_last = bcum[..., -1]
        g = b_last[..., None] - bcum + li
        m_new = jnp.maximum(b_last + m_st, jnp.max(g, axis=-1))
        w = jnp.exp(g - m_new[..., None])
        decay = jnp.exp(b_last + m_st - m_new)
        c_new = decay[..., None, None] * c_st + jnp.einsum('bhsd,bhsv->bhdv', kk * w[..., None], vv)
        n_new = decay[..., None] * n_st + jnp.einsum('bhs,bhsd->bhd', w, kk)
        return (c_new, n_new, m_new), h

    init = (jnp.zeros((bsz, N_HEADS, DK, DV), f32),
            jnp.zeros((bsz, N_HEADS, DK), f32),
            jnp.full((bsz, N_HEADS), -jnp.inf, f32))
    _, hs = lax.scan(step, init, (qc, kc, vc, lic, lfc))
    return hs.transpose(1, 0, 3, 2, 4).reshape(bsz, s, N_HEADS, DV)


def setup_inputs(seed: int = 0) -> dict:
    key = jax.random.key(seed)
    ks = jax.random.split(key, 16)
    nrm = jax.random.normal
    f32 = jnp.float32
    x = nrm(ks[0], (BATCH, SEQ, D_MODEL), f32)
    p = nrm(ks[1], (DEPTH, BATCH, SEQ, PLE_DIM), f32)
    g_mix = 1.0 + 0.02 * nrm(ks[2], (DEPTH, D_MODEL), f32)
    w_in = nrm(ks[3], (DEPTH, D_MODEL, N_IN), f32) * D_MODEL ** -0.5
    conv_w = nrm(ks[4], (DEPTH, CONV_K, CONV_WIDTH), f32) * CONV_K ** -0.5
    conv_b = 0.02 * nrm(ks[5], (DEPTH, CONV_WIDTH), f32)
    w_a_out = nrm(ks[6], (DEPTH, CONV_WIDTH, D_MODEL), f32) * CONV_WIDTH ** -0.5
    gb_noise = 0.1 * nrm(ks[7], (DEPTH, 2 * N_HEADS), f32)
    f_offset = jnp.concatenate([jnp.zeros((N_HEADS,), f32), jnp.linspace(3.0, 6.0, N_HEADS, dtype=f32)])
    b_gates = gb_noise + f_offset
    g_head = 1.0 + 0.02 * nrm(ks[8], (DEPTH, V_DIM), f32)
    w_b_out = nrm(ks[9], (DEPTH, V_DIM, D_MODEL), f32) * V_DIM ** -0.5
    w_o = nrm(ks[10], (DEPTH, D_MODEL, D_MODEL), f32) * D_MODEL ** -0.5
    g_ple = 1.0 + 0.02 * nrm(ks[11], (DEPTH, D_MODEL), f32)
    w_ple_gate = nrm(ks[12], (DEPTH, D_MODEL, D_MODEL), f32) * D_MODEL ** -0.5
    w_ple = nrm(ks[13], (DEPTH, PLE_DIM, D_MODEL), f32) * PLE_DIM ** -0.5
    g_final = 1.0 + 0.02 * nrm(ks[14], (D_MODEL,), f32)
    return {"x": x, "p": p, "g_mix": g_mix, "w_in": w_in, "conv_w": conv_w,
            "conv_b": conv_b, "w_a_out": w_a_out, "b_gates": b_gates,
            "g_head": g_head, "w_b_out": w_b_out, "w_o": w_o, "g_ple": g_ple,
            "w_ple_gate": w_ple_gate, "w_ple": w_ple, "g_final": g_final}


def reference(x, p, g_mix, w_in, conv_w, conv_b, w_a_out, b_gates, g_head,
              w_b_out, w_o, g_ple, w_ple_gate, w_ple, g_final):
    bsz, s = x.shape[0], x.shape[1]
    for l in range(DEPTH):
        hn = rmsnorm(x, g_mix[l])
        proj = jnp.einsum('bsd,de->bse', hn, w_in[l])
        (xa, ba, ca, za, q, k, v, o, zb, ig, fg, ga, gb) = jnp.split(proj, SPLIT_POINTS, axis=-1)

        ya = ba * causal_short_conv(ca * xa, conv_w[l], conv_b[l])
        ya = jnp.einsum('bsc,cd->bsd', ya * jax.nn.silu(za), w_a_out[l])

        ig = ig + b_gates[l, :N_HEADS]
        fg = fg + b_gates[l, N_HEADS:]
        hb = mlstm_chunkwise(q.reshape(bsz, s, N_HEADS, DK), k.reshape(bsz, s, N_HEADS, DK),
                             v.reshape(bsz, s, N_HEADS, DV), ig, fg)
        hb = rmsnorm(hb, g_head[l].reshape(N_HEADS, DV)).reshape(bsz, s, V_DIM).astype(x.dtype)
        yb = jax.nn.sigmoid(o) * hb * jax.nn.silu(zb)
        yb = jnp.einsum('bsv,vd->bsd', yb, w_b_out[l])

        merged = jax.nn.sigmoid(ga) * ya + jax.nn.sigmoid(gb) * yb
        x = x + jnp.einsum('bsd,de->bse', merged, w_o[l])

        gate = jax.nn.sigmoid(jnp.einsum('bsd,de->bse', rmsnorm(x, g_ple[l]), w_ple_gate[l]))
        x = x + gate * jnp.einsum('bsp,pd->bsd', p[l], w_ple[l])
    return rmsnorm(x, g_final)
```

```python
import functools

import jax
import jax.numpy as jnp
from jax import lax
from jax.experimental import pallas as pl
from jax.experimental.pallas import tpu as pltpu

D_MODEL = 1024
PLE_DIM = 256
CONV_WIDTH = 1024
CONV_K = 3
N_HEADS = 4
QK_DIM = 1024
V_DIM = 2048
DK = QK_DIM // N_HEADS
DV = V_DIM // N_HEADS
EPS = 1e-6

LANES = 128
SUBLANES = 8
TILE = 256
DVX = DV + LANES
CONV_BLK = 512
VMEM_LIMIT_BYTES = 60 * 1024 * 1024

BF16 = jnp.bfloat16
F32 = jnp.float32


def _dot(a, b):
    return jnp.dot(a, b, preferred_element_type=F32)


def _sigmoid(v):
    return jax.nn.sigmoid(v)


def _silu(v):
    return v * _sigmoid(v)


def _rms(v, g):
    return v * lax.rsqrt(jnp.mean(v * v, axis=-1, keepdims=True) + EPS) * g


def _block_kernel(x_ref, p_ref, gmix_ref, wconv_ref, wq_ref, wkt_ref, wv_ref, wog_ref, wzb_ref,
                  wgab_ref, wgates_ref, bgates_ref, convw_ref, convb_ref, waout_ref, ghead_ref,
                  wbout_ref, wo_ref, gple_ref, wpg_ref, wple_ref, gfinal_ref,
                  out_ref,
                  c_ref, m_ref, u_ref, hn_ref, ya_ref, yb_ref):
    t = pl.program_id(1)

    @pl.when(t == 0)
    def _():
        c_ref[...] = jnp.zeros_like(c_ref)
        m_ref[...] = jnp.full_like(m_ref, -jnp.inf)
        u_ref[0:SUBLANES, :] = jnp.zeros((SUBLANES, CONV_WIDTH), F32)

    x = x_ref[...]
    hn_ref[...] = _rms(x, gmix_ref[...]).astype(BF16)

    for cb in range(CONV_WIDTH // CONV_BLK):
        c0 = cb * CONV_BLK
        cs = slice(c0, c0 + CONV_BLK)
        hn = hn_ref[...]
        xa = _dot(hn, wconv_ref[:, 0 * CONV_WIDTH + c0:0 * CONV_WIDTH + c0 + CONV_BLK])
        ca = _dot(hn, wconv_ref[:, 2 * CONV_WIDTH + c0:2 * CONV_WIDTH + c0 + CONV_BLK])
        u_ref[SUBLANES:SUBLANES + TILE, cs] = ca * xa
        u0 = u_ref[SUBLANES:SUBLANES + TILE, cs]
        u1 = u_ref[SUBLANES - 1:SUBLANES - 1 + TILE, cs]
        u2 = u_ref[SUBLANES - 2:SUBLANES - 2 + TILE, cs]
        conv = (u2 * convw_ref[0:1, cs] + u1 * convw_ref[1:2, cs] + u0 * convw_ref[2:3, cs]
                + convb_ref[:, cs])
        u_ref[0:SUBLANES, cs] = u_ref[TILE:TILE + SUBLANES, cs]
        ba = _dot(hn, wconv_ref[:, 1 * CONV_WIDTH + c0:1 * CONV_WIDTH + c0 + CONV_BLK])
        za = _dot(hn, wconv_ref[:, 3 * CONV_WIDTH + c0:3 * CONV_WIDTH + c0 + CONV_BLK])
        ya_blk = (ba * conv * _silu(za)).astype(BF16)
        contrib = _dot(ya_blk, waout_ref[cs, :])
        if cb == 0:
            ya_ref[...] = contrib
        else:
            ya_ref[...] += contrib

    hn = hn_ref[...]
    gates = _dot(hn, wgates_ref[...]) + bgates_ref[...]
    lf = jnp.minimum(gates, 0.0) - jnp.log1p(jnp.exp(-jnp.abs(gates)))
    lf_hi = lf.astype(BF16)
    lf_lo = (lf - lf_hi.astype(F32)).astype(BF16)
    row_i = lax.broadcasted_iota(jnp.int32, (TILE, TILE), 0)
    col_i = lax.broadcasted_iota(jnp.int32, (TILE, TILE), 1)
    causal = row_i >= col_i
    tri = causal.astype(BF16)
    bcum = _dot(tri, lf_hi) + _dot(tri, lf_lo)
    lane = lax.broadcasted_iota(jnp.int32, (TILE, LANES), 1)
    gcol = jnp.where(lane < N_HEADS, gates, bcum)
    grow = gcol.T

    for h in range(N_HEADS):
        hn = hn_ref[...]
        q = _dot(hn, wq_ref[:, h * DK:(h + 1) * DK]).astype(BF16)
        kt = lax.dot_general(wkt_ref[h * DK:(h + 1) * DK, :], hn,
                             (((1,), (1,)), ((), ())), preferred_element_type=F32)
        v = _dot(hn, wv_ref[:, h * DV:(h + 1) * DV]).astype(BF16)
        vx = jnp.concatenate([v, jnp.ones((TILE, LANES), BF16)], axis=1)

        li_row = grow[h:h + 1, :]
        b_row = grow[N_HEADS + h:N_HEADS + h + 1, :]
        b_col = gcol[:, N_HEADS + h:N_HEADS + h + 1]
        m_st = m_ref[h, 0:1, 0:1]

        s = _dot(q, kt.astype(BF16))
        dmat = jnp.where(causal, b_col - b_row + li_row, -jnp.inf)
        a = b_col + m_st
        m_row = jnp.maximum(a, jnp.max(dmat, axis=1, keepdims=True))
        pmat = (s * jnp.exp(dmat - m_row)).astype(BF16)
        inter = jnp.exp(a - m_row)
        cx = c_ref[h]
        numx = _dot(pmat, vx) + inter * _dot(q, cx.astype(BF16))
        den = numx[:, DV:DV + 1]
        hh = numx[:, 0:DV] / jnp.maximum(jnp.abs(den), jnp.exp(-m_row))

        b_last = b_row[:, TILE - 1:TILE]
        g_row = b_last - b_row + li_row
        m_new = jnp.maximum(b_last + m_st, jnp.max(g_row, axis=1, keepdims=True))
        w_row = jnp.exp(g_row - m_new)
        decay = jnp.exp(b_last + m_st - m_new)
        ktw = (kt * w_row).astype(BF16)
        c_ref[h] = decay * cx + _dot(ktw, vx)
        m_ref[h] = jnp.broadcast_to(m_new, (SUBLANES, LANES))

        hb = _rms(hh, ghead_ref[:, h * DV:(h + 1) * DV])
        og = _dot(hn, wog_ref[:, h * DV:(h + 1) * DV])
        zb = _dot(hn, wzb_ref[:, h * DV:(h + 1) * DV])
        yb_blk = (_sigmoid(og) * hb * _silu(zb)).astype(BF16)
        contrib = _dot(yb_blk, wbout_ref[h * DV:(h + 1) * DV, :])
        if h == 0:
            yb_ref[...] = contrib
        else:
            yb_ref[...] += contrib

    hn = hn_ref[...]
    ga = _dot(hn, wgab_ref[:, 0:D_MODEL])
    gb = _dot(hn, wgab_ref[:, D_MODEL:2 * D_MODEL])
    merged = (_sigmoid(ga) * ya_ref[...] + _sigmoid(gb) * yb_ref[...]).astype(BF16)
    x1 = x_ref[...] + _dot(merged, wo_ref[...])
    gate = _sigmoid(_dot(_rms(x1, gple_ref[...]).astype(BF16), wpg_ref[...]))
    x2 = x1 + gate * _dot(p_ref[...].astype(BF16), wple_ref[...])
    out_ref[...] = _rms(x2, gfinal_ref[...])


def _const_spec(shape):
    return pl.BlockSpec(shape, lambda b, t: (0,) * len(shape), pipeline_mode=pl.Buffered(1))


def kernel(x, p, g_mix, w_in, conv_w, conv_b, w_a_out, b_gates, g_head, w_b_out, w_o, g_ple,
           w_ple_gate, w_ple, g_final):
    bsz, seq, _ = x.shape
    assert w_in.shape[0] == 1 and seq % TILE == 0
    wi = w_in[0]
    o = 0
    w_conv = wi[:, o:o + 4 * CONV_WIDTH].astype(BF16); o += 4 * CONV_WIDTH
    w_q = (wi[:, o:o + QK_DIM] * (DK ** -0.5)).astype(BF16); o += QK_DIM
    w_kt = wi[:, o:o + QK_DIM].T.astype(BF16); o += QK_DIM
    w_v = wi[:, o:o + V_DIM].astype(BF16); o += V_DIM
    w_og = wi[:, o:o + V_DIM].astype(BF16); o += V_DIM
    w_zb = wi[:, o:o + V_DIM].astype(BF16); o += V_DIM
    w_gates = jnp.pad(wi[:, o:o + 2 * N_HEADS], ((0, 0), (0, LANES - 2 * N_HEADS))).astype(BF16)
    o += 2 * N_HEADS
    w_gab = wi[:, o:o + 2 * D_MODEL].astype(BF16)
    b_g = jnp.pad(b_gates[0], (0, LANES - 2 * N_HEADS)).reshape(1, LANES)

    args = (
        x, p[0], g_mix[0].reshape(1, D_MODEL), w_conv, w_q, w_kt, w_v, w_og, w_zb, w_gab, w_gates, b_g,
        conv_w[0], conv_b[0].reshape(1, CONV_WIDTH), w_a_out[0].astype(BF16),
        g_head[0].reshape(1, V_DIM), w_b_out[0].astype(BF16), w_o[0].astype(BF16),
        g_ple[0].reshape(1, D_MODEL), w_ple_gate[0].astype(BF16), w_ple[0].astype(BF16),
        g_final.reshape(1, D_MODEL),
    )
    in_specs = [
        pl.BlockSpec((None, TILE, D_MODEL), lambda b, t: (b, t, 0)),
        pl.BlockSpec((None, TILE, PLE_DIM), lambda b, t: (b, t, 0)),
    ] + [_const_spec(a.shape) for a in args[2:]]

    return pl.pallas_call(
        _block_kernel,
        grid=(bsz, seq // TILE),
        in_specs=in_specs,
        out_specs=pl.BlockSpec((None, TILE, D_MODEL), lambda b, t: (b, t, 0)),
        out_shape=jax.ShapeDtypeStruct(x.shape, x.dtype),
        scratch_shapes=[
            pltpu.VMEM((N_HEADS, DK, DVX), F32),
            pltpu.VMEM((N_HEADS, SUBLANES, LANES), F32),
            pltpu.VMEM((TILE + SUBLANES, CONV_WIDTH), F32),
            pltpu.VMEM((TILE, D_MODEL), BF16),
            pltpu.VMEM((TILE, D_MODEL), F32),
            pltpu.VMEM((TILE, D_MODEL), F32),
        ],
        compiler_params=pltpu.CompilerParams(
            dimension_semantics=("arbitrary", "arbitrary"),
            vmem_limit_bytes=VMEM_LIMIT_BYTES,
        ),
        name="hybrid_block",
    )(*args)
```

```python
import functools

import jax
import jax.numpy as jnp
from jax import lax
from jax.experimental import pallas as pl
from jax.experimental.pallas import tpu as pltpu

D_MODEL = 1024
PLE_DIM = 256
CONV_WIDTH = 1024
CONV_K = 3
N_HEADS = 4
QK_DIM = 1024
V_DIM = 2048
DK = QK_DIM // N_HEADS
DV = V_DIM // N_HEADS
EPS = 1e-6

LANES = 128
SUBLANES = 8
TILE = 256
DVX = DV + LANES
CONV_BLK = 512
VMEM_LIMIT_BYTES = 60 * 1024 * 1024

BF16 = jnp.bfloat16
F32 = jnp.float32


def _dot(a, b):
    return jnp.dot(a, b, preferred_element_type=F32)


def _sigmoid(v):
    return 0.5 * jnp.tanh(0.5 * v) + 0.5


def _silu(v):
    return v * _sigmoid(v)


def _rms(v, g):
    return v * lax.rsqrt(jnp.mean(v * v, axis=-1, keepdims=True) + EPS) * g


def _block_kernel(tiles_per_row,
                  x_ref, xp_ref, pp_ref, gmix_ref, wconv_ref, wq_ref, wkt_ref, wv_ref, wog_ref,
                  wzb_ref, wgab_ref, wgates_ref, bgates_ref, convw_ref, convb_ref, waout_ref,
                  ghead_ref, wbout_ref, wo_ref, gple_ref, wpg_ref, wple_ref, gfinal_ref,
                  out_ref,
                  c_ref, m_ref, u_ref, hn_ref, ya_ref, yb_ref, ga_ref, gb_ref):
    step = pl.program_id(0)
    cur = jnp.minimum(step, pl.num_programs(0) - 2)

    @pl.when(step == 0)
    def _():
        for ref in (ya_ref, yb_ref, ga_ref, gb_ref):
            ref[...] = jnp.zeros_like(ref)

    @pl.when(lax.rem(cur, tiles_per_row) == 0)
    def _():
        c_ref[...] = jnp.zeros_like(c_ref)
        m_ref[...] = jnp.full_like(m_ref, -jnp.inf)
        u_ref[0:SUBLANES, :] = jnp.zeros((SUBLANES, CONV_WIDTH), F32)

    def hn():
        return hn_ref[...]

    def conv_in(cb):
        c0 = cb * CONV_BLK
        cs = slice(c0, c0 + CONV_BLK)
        xa = _dot(hn(), wconv_ref[:, 0 * CONV_WIDTH + c0:0 * CONV_WIDTH + c0 + CONV_BLK])
        ca = _dot(hn(), wconv_ref[:, 2 * CONV_WIDTH + c0:2 * CONV_WIDTH + c0 + CONV_BLK])
        u_ref[SUBLANES:SUBLANES + TILE, cs] = ca * xa
        u0 = u_ref[SUBLANES:SUBLANES + TILE, cs]
        u1 = u_ref[SUBLANES - 1:SUBLANES - 1 + TILE, cs]
        u2 = u_ref[SUBLANES - 2:SUBLANES - 2 + TILE, cs]
        conv = (u2 * convw_ref[0:1, cs] + u1 * convw_ref[1:2, cs] + u0 * convw_ref[2:3, cs]
                + convb_ref[:, cs])
        u_ref[0:SUBLANES, cs] = u_ref[TILE:TILE + SUBLANES, cs]
        return conv

    def conv_gate(cb, conv):
        c0 = cb * CONV_BLK
        ba = _dot(hn(), wconv_ref[:, 1 * CONV_WIDTH + c0:1 * CONV_WIDTH + c0 + CONV_BLK])
        za = _dot(hn(), wconv_ref[:, 3 * CONV_WIDTH + c0:3 * CONV_WIDTH + c0 + CONV_BLK])
        return (ba * conv * _silu(za)).astype(BF16)

    def conv_out(cb, ya_blk):
        contrib = _dot(ya_blk, waout_ref[cb * CONV_BLK:(cb + 1) * CONV_BLK, :])
        if cb == 0:
            ya_ref[...] = contrib
        else:
            ya_ref[...] += contrib

    row_i = lax.broadcasted_iota(jnp.int32, (TILE, TILE), 0)
    col_i = lax.broadcasted_iota(jnp.int32, (TILE, TILE), 1)
    causal = row_i >= col_i

    def gate_logs():
        gates = _dot(hn(), wgates_ref[...]) + bgates_ref[...]
        lf = jnp.minimum(gates, 0.0) - jnp.log1p(jnp.exp(-jnp.abs(gates)))
        lf_hi = lf.astype(BF16)
        lf_lo = (lf - lf_hi.astype(F32)).astype(BF16)
        tri = causal.astype(BF16)
        bcum = _dot(tri, lf_hi) + _dot(tri, lf_lo)
        lane = lax.broadcasted_iota(jnp.int32, (TILE, LANES), 1)
        gcol = jnp.where(lane < N_HEADS, gates, bcum)
        return gcol, gcol.T

    def head_decay(h, gcol, grow):
        li_row = grow[h:h + 1, :]
        b_row = grow[N_HEADS + h:N_HEADS + h + 1, :]
        b_col = gcol[:, N_HEADS + h:N_HEADS + h + 1]
        m_st = m_ref[h, 0:1, 0:1]
        dmat = jnp.where(causal, b_col - b_row + li_row, -jnp.inf)
        a = b_col + m_st
        m_row = jnp.maximum(a, jnp.max(dmat, axis=1, keepdims=True))
        b_last = b_row[:, TILE - 1:TILE]
        g_row = b_last - b_row + li_row
        m_new = jnp.maximum(b_last + m_st, jnp.max(g_row, axis=1, keepdims=True))
        m_ref[h] = jnp.broadcast_to(m_new, (SUBLANES, LANES))
        return dict(
            dexp=jnp.exp(dmat - m_row),
            inter=jnp.exp(a - m_row),
            floor=jnp.exp(-m_row),
            w_row=jnp.exp(g_row - m_new),
            decay=jnp.exp(b_last + m_st - m_new),
        )

    def head_proj(h):
        q = _dot(hn(), wq_ref[:, h * DK:(h + 1) * DK])
        kt = lax.dot_general(wkt_ref[h * DK:(h + 1) * DK, :], hn(),
                             (((1,), (1,)), ((), ())), preferred_element_type=F32)
        v = _dot(hn(), wv_ref[:, h * DV:(h + 1) * DV]).astype(BF16)
        vx = jnp.concatenate([v, jnp.ones((TILE, LANES), BF16)], axis=1)
        return q, kt, vx

    def head_scores(q, kt):
        return _dot(q.astype(BF16), kt.astype(BF16))

    def head_num(h, q, vx, s, dk):
        pmat = (s * dk["dexp"]).astype(BF16)
        qs = (q * dk["inter"]).astype(BF16)
        numx = _dot(jnp.concatenate([pmat, qs], axis=1),
                    jnp.concatenate([vx, c_ref[h].astype(BF16)], axis=0))
        den = numx[:, DV:DV + 1]
        return numx[:, 0:DV] / jnp.maximum(jnp.abs(den), dk["floor"])

    def head_upd(h, kt, vx, dk):
        ktw = (kt * dk["w_row"]).astype(BF16)
        c_ref[h] = dk["decay"] * c_ref[h] + _dot(ktw, vx)

    def head_gate(h):
        og = _dot(hn(), wog_ref[:, h * DV:(h + 1) * DV])
        zb = _dot(hn(), wzb_ref[:, h * DV:(h + 1) * DV])
        return _sigmoid(og) * _silu(zb)

    def head_out(h, hh, gate):
        yb_blk = (_rms(hh, ghead_ref[:, h * DV:(h + 1) * DV]) * gate).astype(BF16)
        contrib = _dot(yb_blk, wbout_ref[h * DV:(h + 1) * DV, :])
        if h == 0:
            yb_ref[...] = contrib
        else:
            yb_ref[...] += contrib

    pe = _dot(pp_ref[...].astype(BF16), wple_ref[...])
    hn_ref[...] = _rms(x_ref[...], gmix_ref[...]).astype(BF16)
    gcol, grow = gate_logs()
    conv0 = conv_in(0)
    merged = (_sigmoid(ga_ref[...]) * ya_ref[...] + _sigmoid(gb_ref[...]) * yb_ref[...]).astype(BF16)
    ya0 = conv_gate(0, conv0)
    dk0 = head_decay(0, gcol, grow)
    x1 = xp_ref[...] + _dot(merged, wo_ref[...])
    r1 = _rms(x1, gple_ref[...]).astype(BF16)
    conv1 = conv_in(1)
    conv_out(0, ya0)
    dk1 = head_decay(1, gcol, grow)
    ple_gate = _sigmoid(_dot(r1, wpg_ref[...]))
    out_ref[...] = _rms(x1 + ple_gate * pe, gfinal_ref[...])
    ya1 = conv_gate(1, conv1)
    conv_out(1, ya1)

    q0, kt0, vx0 = head_proj(0)
    s0 = head_scores(q0, kt0)
    q1, kt1, vx1 = head_proj(1)
    hh0 = head_num(0, q0, vx0, s0, dk0)
    s1 = head_scores(q1, kt1)
    dk2 = head_decay(2, gcol, grow)
    gate0 = head_gate(0)
    head_upd(0, kt0, vx0, dk0)
    q2, kt2, vx2 = head_proj(2)
    head_out(0, hh0, gate0)
    hh1 = head_num(1, q1, vx1, s1, dk1)
    s2 = head_scores(q2, kt2)
    dk3 = head_decay(3, gcol, grow)
    gate1 = head_gate(1)
    head_upd(1, kt1, vx1, dk1)
    q3, kt3, vx3 = head_proj(3)
    head_out(1, hh1, gate1)
    hh2 = head_num(2, q2, vx2, s2, dk2)
    s3 = head_scores(q3, kt3)
    gate2 = head_gate(2)
    head_upd(2, kt2, vx2, dk2)
    head_out(2, hh2, gate2)
    hh3 = head_num(3, q3, vx3, s3, dk3)
    gate3 = head_gate(3)
    ga_ref[...] = _dot(hn(), wgab_ref[:, 0:D_MODEL])
    head_upd(3, kt3, vx3, dk3)
    head_out(3, hh3, gate3)
    gb_ref[...] = _dot(hn(), wgab_ref[:, D_MODEL:2 * D_MODEL])


def _const_spec(shape):
    return pl.BlockSpec(shape, lambda s: (0,) * len(shape), pipeline_mode=pl.Buffered(1))


def kernel(x, p, g_mix, w_in, conv_w, conv_b, w_a_out, b_gates, g_head, w_b_out, w_o, g_ple,
           w_ple_gate, w_ple, g_final):
    bsz, seq, _ = x.shape
    assert w_in.shape[0] == 1 and seq % TILE == 0
    tiles_per_row = seq // TILE
    n_tiles = bsz * tiles_per_row
    wi = w_in[0]
    o = 0
    w_conv = wi[:, o:o + 4 * CONV_WIDTH].astype(BF16); o += 4 * CONV_WIDTH
    w_q = (wi[:, o:o + QK_DIM] * (DK ** -0.5)).astype(BF16); o += QK_DIM
    w_kt = wi[:, o:o + QK_DIM].T.astype(BF16); o += QK_DIM
    w_v = wi[:, o:o + V_DIM].astype(BF16); o += V_DIM
    w_og = wi[:, o:o + V_DIM].astype(BF16); o += V_DIM
    w_zb = wi[:, o:o + V_DIM].astype(BF16); o += V_DIM
    w_gates = jnp.pad(wi[:, o:o + 2 * N_HEADS], ((0, 0), (0, LANES - 2 * N_HEADS))).astype(BF16)
    o += 2 * N_HEADS
    w_gab = wi[:, o:o + 2 * D_MODEL].astype(BF16)
    b_g = jnp.pad(b_gates[0], (0, LANES - 2 * N_HEADS)).reshape(1, LANES)

    consts = (
        g_mix[0].reshape(1, D_MODEL), w_conv, w_q, w_kt, w_v, w_og, w_zb, w_gab, w_gates, b_g,
        conv_w[0], conv_b[0].reshape(1, CONV_WIDTH), w_a_out[0].astype(BF16),
        g_head[0].reshape(1, V_DIM), w_b_out[0].astype(BF16), w_o[0].astype(BF16),
        g_ple[0].reshape(1, D_MODEL), w_ple_gate[0].astype(BF16), w_ple[0].astype(BF16),
        g_final.reshape(1, D_MODEL),
    )

    def cur_map(s):
        c = jnp.minimum(s, n_tiles - 1)
        return (c // tiles_per_row, c % tiles_per_row, 0)

    def prev_map(s):
        c = jnp.maximum(s - 1, 0)
        return (c // tiles_per_row, c % tiles_per_row, 0)

    in_specs = [
        pl.BlockSpec((None, TILE, D_MODEL), cur_map),
        pl.BlockSpec((None, TILE, D_MODEL), prev_map),
        pl.BlockSpec((None, TILE, PLE_DIM), prev_map),
    ] + [_const_spec(a.shape) for a in consts]

    return pl.pallas_call(
        functools.partial(_block_kernel, tiles_per_row),
        grid=(n_tiles + 1,),
        in_specs=in_specs,
        out_specs=pl.BlockSpec((None, TILE, D_MODEL), prev_map),
        out_shape=jax.ShapeDtypeStruct(x.shape, x.dtype),
        scratch_shapes=[
            pltpu.VMEM((N_HEADS, DK, DVX), F32),
            pltpu.VMEM((N_HEADS, SUBLANES, LANES), F32),
            pltpu.VMEM((TILE + SUBLANES, CONV_WIDTH), F32),
            pltpu.VMEM((TILE, D_MODEL), BF16),
            pltpu.VMEM((TILE, D_MODEL), F32),
            pltpu.VMEM((TILE, D_MODEL), F32),
            pltpu.VMEM((TILE, D_MODEL), F32),
            pltpu.VMEM((TILE, D_MODEL), F32),
        ],
        compiler_params=pltpu.CompilerParams(
            dimension_semantics=("arbitrary",),
            vmem_limit_bytes=VMEM_LIMIT_BYTES,
        ),
        name="hybrid_block",
    )(x, x, p[0], *consts)
```

```python
import functools

import jax
import jax.numpy as jnp
from jax import lax
from jax.experimental import pallas as pl
from jax.experimental.pallas import tpu as pltpu

D_MODEL = 1024
PLE_DIM = 256
CONV_WIDTH = 1024
CONV_K = 3
N_HEADS = 4
QK_DIM = 1024
V_DIM = 2048
DK = QK_DIM // N_HEADS
DV = V_DIM // N_HEADS
EPS = 1e-6

LANES = 128
SUBLANES = 8
TILE = 256
DVX = DV + LANES
CONV_BLK = 512
VMEM_LIMIT_BYTES = 60 * 1024 * 1024

BF16 = jnp.bfloat16
F32 = jnp.float32


def _dot(a, b):
    return jnp.dot(a, b, preferred_element_type=F32)


def _sigmoid(v):
    return 0.5 * jnp.tanh(0.5 * v) + 0.5


def _silu(v):
    return v * _sigmoid(v)


def _rms(v, g):
    return v * lax.rsqrt(jnp.mean(v * v, axis=-1, keepdims=True) + EPS) * g


def _block_kernel(tiles_per_row,
                  x_ref, xp_ref, pp_ref, gmix_ref, wconv_ref, wq_ref, wkt_ref, wv_ref, wog_ref,
                  wzb_ref, wgab_ref, wgates_ref, bgates_ref, convw_ref, convb_ref, waout_ref,
                  ghead_ref, wbout_ref, wo_ref, gple_ref, wpg_ref, wple_ref, gfinal_ref,
                  out_ref,
                  c_ref, m_ref, u_ref, hn_ref, hnp_ref, ya_ref, yb_ref):
    step = pl.program_id(0)
    cur = jnp.minimum(step, pl.num_programs(0) - 2)

    @pl.when(step == 0)
    def _():
        for ref in (ya_ref, yb_ref, hnp_ref):
            ref[...] = jnp.zeros_like(ref)

    @pl.when(lax.rem(cur, tiles_per_row) == 0)
    def _():
        c_ref[...] = jnp.zeros_like(c_ref)
        m_ref[...] = jnp.full_like(m_ref, -jnp.inf)
        u_ref[0:SUBLANES, :] = jnp.zeros((SUBLANES, CONV_WIDTH), F32)

    def hn():
        return hn_ref[...]

    def conv_in(cb):
        c0 = cb * CONV_BLK
        cs = slice(c0, c0 + CONV_BLK)
        xa = _dot(hn(), wconv_ref[:, 0 * CONV_WIDTH + c0:0 * CONV_WIDTH + c0 + CONV_BLK])
        ca = _dot(hn(), wconv_ref[:, 2 * CONV_WIDTH + c0:2 * CONV_WIDTH + c0 + CONV_BLK])
        u_ref[SUBLANES:SUBLANES + TILE, cs] = ca * xa
        u0 = u_ref[SUBLANES:SUBLANES + TILE, cs]
        u1 = u_ref[SUBLANES - 1:SUBLANES - 1 + TILE, cs]
        u2 = u_ref[SUBLANES - 2:SUBLANES - 2 + TILE, cs]
        conv = (u2 * convw_ref[0:1, cs] + u1 * convw_ref[1:2, cs] + u0 * convw_ref[2:3, cs]
                + convb_ref[:, cs])
        u_ref[0:SUBLANES, cs] = u_ref[TILE:TILE + SUBLANES, cs]
        return conv

    def conv_gate(cb, conv):
        c0 = cb * CONV_BLK
        ba = _dot(hn(), wconv_ref[:, 1 * CONV_WIDTH + c0:1 * CONV_WIDTH + c0 + CONV_BLK])
        za = _dot(hn(), wconv_ref[:, 3 * CONV_WIDTH + c0:3 * CONV_WIDTH + c0 + CONV_BLK])
        return (ba * conv * _silu(za)).astype(BF16)

    def conv_out(cb, ya_blk):
        contrib = _dot(ya_blk, waout_ref[cb * CONV_BLK:(cb + 1) * CONV_BLK, :])
        if cb == 0:
            ya_ref[...] = contrib
        else:
            ya_ref[...] += contrib

    row_i = lax.broadcasted_iota(jnp.int32, (TILE, TILE), 0)
    col_i = lax.broadcasted_iota(jnp.int32, (TILE, TILE), 1)
    causal = row_i >= col_i

    def gate_proj():
        gates = _dot(hn(), wgates_ref[...]) + bgates_ref[...]
        lf = jnp.minimum(gates, 0.0) - jnp.log1p(jnp.exp(-jnp.abs(gates)))
        lf_hi = lf.astype(BF16)
        lf_lo = (lf - lf_hi.astype(F32)).astype(BF16)
        return gates, lf_hi, lf_lo

    def gate_cumsum(gates, lf_hi, lf_lo):
        tri = causal.astype(BF16)
        bcum = _dot(tri, lf_hi) + _dot(tri, lf_lo)
        lane = lax.broadcasted_iota(jnp.int32, (TILE, LANES), 1)
        gcol = jnp.where(lane < N_HEADS, gates, bcum)
        return gcol, gcol.T

    def head_decay(h, gcol, grow):
        li_row = grow[h:h + 1, :]
        b_row = grow[N_HEADS + h:N_HEADS + h + 1, :]
        b_col = gcol[:, N_HEADS + h:N_HEADS + h + 1]
        m_st = m_ref[h, 0:1, 0:1]
        dmat = jnp.where(causal, b_col - b_row + li_row, -jnp.inf)
        a = b_col + m_st
        m_row = jnp.maximum(a, jnp.max(dmat, axis=1, keepdims=True))
        b_last = b_row[:, TILE - 1:TILE]
        g_row = b_last - b_row + li_row
        m_new = jnp.maximum(b_last + m_st, jnp.max(g_row, axis=1, keepdims=True))
        m_ref[h] = jnp.broadcast_to(m_new, (SUBLANES, LANES))
        return dict(
            dexp=jnp.exp(dmat - m_row),
            inter=jnp.exp(a - m_row),
            floor=jnp.exp(-m_row),
            w_row=jnp.exp(g_row - m_new),
            decay=jnp.exp(b_last + m_st - m_new),
        )

    def head_proj(h):
        q = _dot(hn(), wq_ref[:, h * DK:(h + 1) * DK])
        kt = lax.dot_general(wkt_ref[h * DK:(h + 1) * DK, :], hn(),
                             (((1,), (1,)), ((), ())), preferred_element_type=F32)
        v = _dot(hn(), wv_ref[:, h * DV:(h + 1) * DV]).astype(BF16)
        vx = jnp.concatenate([v, jnp.ones((TILE, LANES), BF16)], axis=1)
        return q, kt, vx

    def head_scores(q, kt):
        return _dot(q.astype(BF16), kt.astype(BF16))

    def head_num(h, q, vx, s, dk):
        pmat = (s * dk["dexp"]).astype(BF16)
        qs = (q * dk["inter"]).astype(BF16)
        numx = _dot(jnp.concatenate([pmat, qs], axis=1),
                    jnp.concatenate([vx, c_ref[h].astype(BF16)], axis=0))
        den = numx[:, DV:DV + 1]
        return numx[:, 0:DV] / jnp.maximum(jnp.abs(den), dk["floor"])

    def head_ktw(kt, dk):
        return (kt * dk["w_row"]).astype(BF16)

    def head_upd(h, ktw, vx, dk):
        c_ref[h] = dk["decay"] * c_ref[h] + _dot(ktw, vx)

    def head_gate(h):
        og = _dot(hn(), wog_ref[:, h * DV:(h + 1) * DV])
        zb = _dot(hn(), wzb_ref[:, h * DV:(h + 1) * DV])
        return _sigmoid(og) * _silu(zb)

    def head_out(h, hh, gate):
        yb_blk = (_rms(hh, ghead_ref[:, h * DV:(h + 1) * DV]) * gate).astype(BF16)
        contrib = _dot(yb_blk, wbout_ref[h * DV:(h + 1) * DV, :])
        if h == 0:
            yb_ref[...] = contrib
        else:
            yb_ref[...] += contrib

    hnp = hnp_ref[...]
    ga = _dot(hnp, wgab_ref[:, 0:D_MODEL])
    gb = _dot(hnp, wgab_ref[:, D_MODEL:2 * D_MODEL])
    pe = _dot(pp_ref[...].astype(BF16), wple_ref[...])
    hn_ref[...] = _rms(x_ref[...], gmix_ref[...]).astype(BF16)
    gparts = gate_proj()
    conv0 = conv_in(0)
    gcol, grow = gate_cumsum(*gparts)
    merged = (_sigmoid(ga) * ya_ref[...] + _sigmoid(gb) * yb_ref[...]).astype(BF16)
    ya0 = conv_gate(0, conv0)
    dk0 = head_decay(0, gcol, grow)
    x1 = xp_ref[...] + _dot(merged, wo_ref[...])
    r1 = _rms(x1, gple_ref[...]).astype(BF16)
    conv1 = conv_in(1)
    conv_out(0, ya0)
    dk1 = head_decay(1, gcol, grow)
    ple_gate = _sigmoid(_dot(r1, wpg_ref[...]))
    out_ref[...] = _rms(x1 + ple_gate * pe, gfinal_ref[...])
    ya1 = conv_gate(1, conv1)

    q0, kt0, vx0 = head_proj(0)
    conv_out(1, ya1)
    s0 = head_scores(q0, kt0)
    q1, kt1, vx1 = head_proj(1)
    hh0 = head_num(0, q0, vx0, s0, dk0)
    ktw0 = head_ktw(kt0, dk0)
    s1 = head_scores(q1, kt1)
    dk2 = head_decay(2, gcol, grow)
    gate0 = head_gate(0)
    q2, kt2, vx2 = head_proj(2)
    head_out(0, hh0, gate0)
    hh1 = head_num(1, q1, vx1, s1, dk1)
    ktw1 = head_ktw(kt1, dk1)
    s2 = head_scores(q2, kt2)
    dk3 = head_decay(3, gcol, grow)
    gate1 = head_gate(1)
    q3, kt3, vx3 = head_proj(3)
    head_out(1, hh1, gate1)
    hh2 = head_num(2, q2, vx2, s2, dk2)
    ktw2 = head_ktw(kt2, dk2)
    s3 = head_scores(q3, kt3)
    gate2 = head_gate(2)
    head_out(2, hh2, gate2)
    hh3 = head_num(3, q3, vx3, s3, dk3)
    ktw3 = head_ktw(kt3, dk3)
    gate3 = head_gate(3)
    head_upd(0, ktw0, vx0, dk0)
    head_upd(1, ktw1, vx1, dk1)
    head_upd(2, ktw2, vx2, dk2)
    head_out(3, hh3, gate3)
    head_upd(3, ktw3, vx3, dk3)
    hnp_ref[...] = hn()


def _const_spec(shape):
    return pl.BlockSpec(shape, lambda s: (0,) * len(shape), pipeline_mode=pl.Buffered(1))


def kernel(x, p, g_mix, w_in, conv_w, conv_b, w_a_out, b_gates, g_head, w_b_out, w_o, g_ple,
           w_ple_gate, w_ple, g_final):
    bsz, seq, _ = x.shape
    assert w_in.shape[0] == 1 and seq % TILE == 0
    tiles_per_row = seq // TILE
    n_tiles = bsz * tiles_per_row
    wi = w_in[0]
    o = 0
    w_conv = wi[:, o:o + 4 * CONV_WIDTH].astype(BF16); o += 4 * CONV_WIDTH
    w_q = (wi[:, o:o + QK_DIM] * (DK ** -0.5)).astype(BF16); o += QK_DIM
    w_kt = wi[:, o:o + QK_DIM].T.astype(BF16); o += QK_DIM
    w_v = wi[:, o:o + V_DIM].astype(BF16); o += V_DIM
    w_og = wi[:, o:o + V_DIM].astype(BF16); o += V_DIM
    w_zb = wi[:, o:o + V_DIM].astype(BF16); o += V_DIM
    w_gates = jnp.pad(wi[:, o:o + 2 * N_HEADS], ((0, 0), (0, LANES - 2 * N_HEADS))).astype(BF16)
    o += 2 * N_HEADS
    w_gab = wi[:, o:o + 2 * D_MODEL].astype(BF16)
    b_g = jnp.pad(b_gates[0], (0, LANES - 2 * N_HEADS)).reshape(1, LANES)

    consts = (
        g_mix[0].reshape(1, D_MODEL), w_conv, w_q, w_kt, w_v, w_og, w_zb, w_gab, w_gates, b_g,
        conv_w[0], conv_b[0].reshape(1, CONV_WIDTH), w_a_out[0].astype(BF16),
        g_head[0].reshape(1, V_DIM), w_b_out[0].astype(BF16), w_o[0].astype(BF16),
        g_ple[0].reshape(1, D_MODEL), w_ple_gate[0].astype(BF16), w_ple[0].astype(BF16),
        g_final.reshape(1, D_MODEL),
    )

    def cur_map(s):
        c = jnp.minimum(s, n_tiles - 1)
        return (c // tiles_per_row, c % tiles_per_row, 0)

    def prev_map(s):
        c = jnp.maximum(s - 1, 0)
        return (c // tiles_per_row, c % tiles_per_row, 0)

    in_specs = [
        pl.BlockSpec((None, TILE, D_MODEL), cur_map),
        pl.BlockSpec((None, TILE, D_MODEL), prev_map),
        pl.BlockSpec((None, TILE, PLE_DIM), prev_map),
    ] + [_const_spec(a.shape) for a in consts]

    return pl.pallas_call(
        functools.partial(_block_kernel, tiles_per_row),
        grid=(n_tiles + 1,),
        in_specs=in_specs,
        out_specs=pl.BlockSpec((None, TILE, D_MODEL), prev_map),
        out_shape=jax.ShapeDtypeStruct(x.shape, x.dtype),
        scratch_shapes=[
            pltpu.VMEM((N_HEADS, DK, DVX), F32),
            pltpu.VMEM((N_HEADS, SUBLANES, LANES), F32),
            pltpu.VMEM((TILE + SUBLANES, CONV_WIDTH), F32),
            pltpu.VMEM((TILE, D_MODEL), BF16),
            pltpu.VMEM((TILE, D_MODEL), BF16),
            pltpu.VMEM((TILE, D_MODEL), F32),
            pltpu.VMEM((TILE, D_MODEL), F32),
        ],
        compiler_params=pltpu.CompilerParams(
            dimension_semantics=("arbitrary",),
            vmem_limit_bytes=VMEM_LIMIT_BYTES,
        ),
        name="hybrid_block",
    )(x, x, p[0], *consts)
```

```python
import functools

import jax
import jax.numpy as jnp
from jax import lax
from jax.experimental import pallas as pl
from jax.experimental.pallas import tpu as pltpu

D_MODEL = 1024
PLE_DIM = 256
CONV_WIDTH = 1024
CONV_K = 3
N_HEADS = 4
QK_DIM = 1024
V_DIM = 2048
DK = QK_DIM // N_HEADS
DV = V_DIM // N_HEADS
EPS = 1e-6

LANES = 128
SUBLANES = 8
TILE = 256
DVX = DV + LANES
CONV_BLK = 512
W_ROWS = 1024
W_KBLK = 256
VMEM_LIMIT_BYTES = 60 * 1024 * 1024

BF16 = jnp.bfloat16
F32 = jnp.float32


def _dot(a, b):
    return jnp.dot(a, b, preferred_element_type=F32)


def _sigmoid(v):
    return 0.5 * jnp.tanh(0.5 * v) + 0.5


def _silu(v):
    return v * _sigmoid(v)


def _rms(v, g):
    return v * lax.rsqrt(jnp.mean(v * v, axis=-1, keepdims=True) + EPS) * g


def _load_input_weights(wt_hbm, wconv_ref, wq_ref, wkt_ref, wv_ref, wog_ref, wzb_ref, wgab_ref,
                        wgates_ref, stage_ref, gstage_ref, sem):
    def fetch(src, dst):
        cp = pltpu.make_async_copy(src, dst, sem)
        cp.start()
        cp.wait()

    def load_transposed(row0, dst_ref, col0, scale):
        def body(j, carry):
            k0 = pl.multiple_of(j * W_KBLK, W_KBLK)
            fetch(wt_hbm.at[pl.ds(row0, W_ROWS), pl.ds(k0, W_KBLK)], stage_ref)
            blk = stage_ref[...].T
            if scale is not None:
                blk = blk * scale
            dst_ref[pl.ds(k0, W_KBLK), col0:col0 + W_ROWS] = blk.astype(BF16)
            return carry
        lax.fori_loop(0, D_MODEL // W_KBLK, body, 0)

    o = 0
    for g in range(4 * CONV_WIDTH // W_ROWS):
        load_transposed(o + g * W_ROWS, wconv_ref, g * W_ROWS, None)
    o += 4 * CONV_WIDTH
    load_transposed(o, wq_ref, 0, DK ** -0.5)
    o += QK_DIM
    for j in range(D_MODEL // W_KBLK):
        fetch(wt_hbm.at[pl.ds(o, W_ROWS), pl.ds(j * W_KBLK, W_KBLK)], stage_ref)
        wkt_ref[:, j * W_KBLK:(j + 1) * W_KBLK] = stage_ref[...].astype(BF16)
    o += QK_DIM
    for dst_ref in (wv_ref, wog_ref, wzb_ref):
        for g in range(V_DIM // W_ROWS):
            load_transposed(o + g * W_ROWS, dst_ref, g * W_ROWS, None)
        o += V_DIM
    fetch(wt_hbm.at[pl.ds(o, 2 * N_HEADS), :], gstage_ref)
    gpad = jnp.concatenate([gstage_ref[...], jnp.zeros((LANES - 2 * N_HEADS, D_MODEL), F32)], axis=0)
    wgates_ref[...] = gpad.T.astype(BF16)
    o += 2 * N_HEADS
    for g in range(2 * D_MODEL // W_ROWS):
        load_transposed(o + g * W_ROWS, wgab_ref, g * W_ROWS, None)


def _block_kernel(tiles_per_row,
                  x_ref, xp_ref, pp_ref, wt_hbm, gmix_ref, bgates_ref, convw_ref, convb_ref,
                  waout_ref, ghead_ref, wbout_ref, wo_ref, gple_ref, wpg_ref, wple_ref, gfinal_ref,
                  out_ref,
                  c_ref, m_ref, u_ref, hn_ref, hnp_ref, ya_ref, yb_ref,
                  wconv_ref, wq_ref, wkt_ref, wv_ref, wog_ref, wzb_ref, wgab_ref, wgates_ref,
                  stage_ref, gstage_ref, dma_sem):
    step = pl.program_id(0)
    cur = jnp.minimum(step, pl.num_programs(0) - 2)

    @pl.when(step == 0)
    def _():
        for ref in (ya_ref, yb_ref, hnp_ref):
            ref[...] = jnp.zeros_like(ref)
        _load_input_weights(wt_hbm, wconv_ref, wq_ref, wkt_ref, wv_ref, wog_ref, wzb_ref, wgab_ref,
                            wgates_ref, stage_ref, gstage_ref, dma_sem)

    @pl.when(lax.rem(cur, tiles_per_row) == 0)
    def _():
        c_ref[...] = jnp.zeros_like(c_ref)
        m_ref[...] = jnp.full_like(m_ref, -jnp.inf)
        u_ref[0:SUBLANES, :] = jnp.zeros((SUBLANES, CONV_WIDTH), F32)

    def hn():
        return hn_ref[...]

    def conv_in(cb):
        c0 = cb * CONV_BLK
        cs = slice(c0, c0 + CONV_BLK)
        xa = _dot(hn(), wconv_ref[:, 0 * CONV_WIDTH + c0:0 * CONV_WIDTH + c0 + CONV_BLK])
        ca = _dot(hn(), wconv_ref[:, 2 * CONV_WIDTH + c0:2 * CONV_WIDTH + c0 + CONV_BLK])
        u_ref[SUBLANES:SUBLANES + TILE, cs] = ca * xa
        u0 = u_ref[SUBLANES:SUBLANES + TILE, cs]
        u1 = u_ref[SUBLANES - 1:SUBLANES - 1 + TILE, cs]
        u2 = u_ref[SUBLANES - 2:SUBLANES - 2 + TILE, cs]
        conv = (u2 * convw_ref[0:1, cs] + u1 * convw_ref[1:2, cs] + u0 * convw_ref[2:3, cs]
                + convb_ref[:, cs])
        u_ref[0:SUBLANES, cs] = u_ref[TILE:TILE + SUBLANES, cs]
        return conv

    def conv_gate(cb, conv):
        c0 = cb * CONV_BLK
        ba = _dot(hn(), wconv_ref[:, 1 * CONV_WIDTH + c0:1 * CONV_WIDTH + c0 + CONV_BLK])
        za = _dot(hn(), wconv_ref[:, 3 * CONV_WIDTH + c0:3 * CONV_WIDTH + c0 + CONV_BLK])
        return (ba * conv * _silu(za)).astype(BF16)

    def conv_out(cb, ya_blk):
        contrib = _dot(ya_blk, waout_ref[cb * CONV_BLK:(cb + 1) * CONV_BLK, :])
        if cb == 0:
            ya_ref[...] = contrib
        else:
            ya_ref[...] += contrib

    row_i = lax.broadcasted_iota(jnp.int32, (TILE, TILE), 0)
    col_i = lax.broadcasted_iota(jnp.int32, (TILE, TILE), 1)
    causal = row_i >= col_i

    def gate_proj():
        gates = _dot(hn(), wgates_ref[...]) + bgates_ref[...]
        lf = jnp.minimum(gates, 0.0) - jnp.log1p(jnp.exp(-jnp.abs(gates)))
        lf_hi = lf.astype(BF16)
        lf_lo = (lf - lf_hi.astype(F32)).astype(BF16)
        return gates, lf_hi, lf_lo

    def gate_cumsum(gates, lf_hi, lf_lo):
        tri = causal.astype(BF16)
        bcum = _dot(tri, lf_hi) + _dot(tri, lf_lo)
        lane = lax.broadcasted_iota(jnp.int32, (TILE, LANES), 1)
        gcol = jnp.where(lane < N_HEADS, gates, bcum)
        return gcol, gcol.T

    def head_decay(h, gcol, grow):
        li_row = grow[h:h + 1, :]
        b_row = grow[N_HEADS + h:N_HEADS + h + 1, :]
        b_col = gcol[:, N_HEADS + h:N_HEADS + h + 1]
        m_st = m_ref[h, 0:1, 0:1]
        dmat = jnp.where(causal, b_col - b_row + li_row, -jnp.inf)
        a = b_col + m_st
        m_row = jnp.maximum(a, jnp.max(dmat, axis=1, keepdims=True))
        b_last = b_row[:, TILE - 1:TILE]
        g_row = b_last - b_row + li_row
        m_new = jnp.maximum(b_last + m_st, jnp.max(g_row, axis=1, keepdims=True))
        m_ref[h] = jnp.broadcast_to(m_new, (SUBLANES, LANES))
        return dict(
            dexp=jnp.exp(dmat - m_row),
            inter=jnp.exp(a - m_row),
            floor=jnp.exp(-m_row),
            w_row=jnp.exp(g_row - m_new),
            decay=jnp.exp(b_last + m_st - m_new),
        )

    def head_proj(h):
        q = _dot(hn(), wq_ref[:, h * DK:(h + 1) * DK])
        kt = lax.dot_general(wkt_ref[h * DK:(h + 1) * DK, :], hn(),
                             (((1,), (1,)), ((), ())), preferred_element_type=F32)
        v = _dot(hn(), wv_ref[:, h * DV:(h + 1) * DV]).astype(BF16)
        vx = jnp.concatenate([v, jnp.ones((TILE, LANES), BF16)], axis=1)
        return q, kt, vx

    def head_scores(q, kt):
        return _dot(q.astype(BF16), kt.astype(BF16))

    def head_num(h, q, vx, s, dk):
        pmat = (s * dk["dexp"]).astype(BF16)
        qs = (q * dk["inter"]).astype(BF16)
        numx = _dot(jnp.concatenate([pmat, qs], axis=1),
                    jnp.concatenate([vx, c_ref[h].astype(BF16)], axis=0))
        den = numx[:, DV:DV + 1]
        return numx[:, 0:DV] / jnp.maximum(jnp.abs(den), dk["floor"])

    def head_ktw(kt, dk):
        return (kt * dk["w_row"]).astype(BF16)

    def head_upd(h, ktw, vx, dk):
        c_ref[h] = dk["decay"] * c_ref[h] + _dot(ktw, vx)

    def head_gate(h):
        og = _dot(hn(), wog_ref[:, h * DV:(h + 1) * DV])
        zb = _dot(hn(), wzb_ref[:, h * DV:(h + 1) * DV])
        return _sigmoid(og) * _silu(zb)

    def head_out(h, hh, gate):
        yb_blk = (_rms(hh, ghead_ref[:, h * DV:(h + 1) * DV]) * gate).astype(BF16)
        contrib = _dot(yb_blk, wbout_ref[h * DV:(h + 1) * DV, :])
        if h == 0:
            yb_ref[...] = contrib
        else:
            yb_ref[...] += contrib

    hnp = hnp_ref[...]
    ga = _dot(hnp, wgab_ref[:, 0:D_MODEL])
    gb = _dot(hnp, wgab_ref[:, D_MODEL:2 * D_MODEL])
    pe = _dot(pp_ref[...].astype(BF16), wple_ref[...])
    hn_ref[...] = _rms(x_ref[...], gmix_ref[...]).astype(BF16)
    gparts = gate_proj()
    conv0 = conv_in(0)
    gcol, grow = gate_cumsum(*gparts)
    merged = (_sigmoid(ga) * ya_ref[...] + _sigmoid(gb) * yb_ref[...]).astype(BF16)
    ya0 = conv_gate(0, conv0)
    dk0 = head_decay(0, gcol, grow)
    x1 = xp_ref[...] + _dot(merged, wo_ref[...])
    r1 = _rms(x1, gple_ref[...]).astype(BF16)
    conv1 = conv_in(1)
    conv_out(0, ya0)
    dk1 = head_decay(1, gcol, grow)
    ple_gate = _sigmoid(_dot(r1, wpg_ref[...]))
    out_ref[...] = _rms(x1 + ple_gate * pe, gfinal_ref[...])
    ya1 = conv_gate(1, conv1)

    q0, kt0, vx0 = head_proj(0)
    conv_out(1, ya1)
    s0 = head_scores(q0, kt0)
    q1, kt1, vx1 = head_proj(1)
    hh0 = head_num(0, q0, vx0, s0, dk0)
    ktw0 = head_ktw(kt0, dk0)
    s1 = head_scores(q1, kt1)
    dk2 = head_decay(2, gcol, grow)
    gate0 = head_gate(0)
    q2, kt2, vx2 = head_proj(2)
    head_out(0, hh0, gate0)
    hh1 = head_num(1, q1, vx1, s1, dk1)
    ktw1 = head_ktw(kt1, dk1)
    s2 = head_scores(q2, kt2)
    dk3 = head_decay(3, gcol, grow)
    gate1 = head_gate(1)
    q3, kt3, vx3 = head_proj(3)
    head_out(1, hh1, gate1)
    hh2 = head_num(2, q2, vx2, s2, dk2)
    ktw2 = head_ktw(kt2, dk2)
    s3 = head_scores(q3, kt3)
    gate2 = head_gate(2)
    head_out(2, hh2, gate2)
    hh3 = head_num(3, q3, vx3, s3, dk3)
    ktw3 = head_ktw(kt3, dk3)
    gate3 = head_gate(3)
    head_upd(0, ktw0, vx0, dk0)
    head_upd(1, ktw1, vx1, dk1)
    head_upd(2, ktw2, vx2, dk2)
    head_out(3, hh3, gate3)
    head_upd(3, ktw3, vx3, dk3)
    hnp_ref[...] = hn()


def _const_spec(shape):
    return pl.BlockSpec(shape, lambda s: (0,) * len(shape), pipeline_mode=pl.Buffered(1))


def kernel(x, p, g_mix, w_in, conv_w, conv_b, w_a_out, b_gates, g_head, w_b_out, w_o, g_ple,
           w_ple_gate, w_ple, g_final):
    bsz, seq, _ = x.shape
    assert w_in.shape[0] == 1 and seq % TILE == 0
    tiles_per_row = seq // TILE
    n_tiles = bsz * tiles_per_row
    assert w_in.shape[2] == 4 * CONV_WIDTH + 2 * QK_DIM + 3 * V_DIM + 2 * N_HEADS + 2 * D_MODEL
    wt = jnp.transpose(w_in[0])
    b_g = jnp.pad(b_gates[0], (0, LANES - 2 * N_HEADS)).reshape(1, LANES)

    consts = (
        g_mix[0].reshape(1, D_MODEL), b_g,
        conv_w[0], conv_b[0].reshape(1, CONV_WIDTH), w_a_out[0].astype(BF16),
        g_head[0].reshape(1, V_DIM), w_b_out[0].astype(BF16), w_o[0].astype(BF16),
        g_ple[0].reshape(1, D_MODEL), w_ple_gate[0].astype(BF16), w_ple[0].astype(BF16),
        g_final.reshape(1, D_MODEL),
    )

    def cur_map(s):
        c = jnp.minimum(s, n_tiles - 1)
        return (c // tiles_per_row, c % tiles_per_row, 0)

    def prev_map(s):
        c = jnp.maximum(s - 1, 0)
        return (c // tiles_per_row, c % tiles_per_row, 0)

    in_specs = [
        pl.BlockSpec((None, TILE, D_MODEL), cur_map),
        pl.BlockSpec((None, TILE, D_MODEL), prev_map),
        pl.BlockSpec((None, TILE, PLE_DIM), prev_map),
        pl.BlockSpec(memory_space=pl.ANY),
    ] + [_const_spec(a.shape) for a in consts]

    return pl.pallas_call(
        functools.partial(_block_kernel, tiles_per_row),
        grid=(n_tiles + 1,),
        in_specs=in_specs,
        out_specs=pl.BlockSpec((None, TILE, D_MODEL), prev_map),
        out_shape=jax.ShapeDtypeStruct(x.shape, x.dtype),
        scratch_shapes=[
            pltpu.VMEM((N_HEADS, DK, DVX), F32),
            pltpu.VMEM((N_HEADS, SUBLANES, LANES), F32),
            pltpu.VMEM((TILE + SUBLANES, CONV_WIDTH), F32),
            pltpu.VMEM((TILE, D_MODEL), BF16),
            pltpu.VMEM((TILE, D_MODEL), BF16),
            pltpu.VMEM((TILE, D_MODEL), F32),
            pltpu.VMEM((TILE, D_MODEL), F32),
            pltpu.VMEM((D_MODEL, 4 * CONV_WIDTH), BF16),
            pltpu.VMEM((D_MODEL, QK_DIM), BF16),
            pltpu.VMEM((QK_DIM, D_MODEL), BF16),
            pltpu.VMEM((D_MODEL, V_DIM), BF16),
            pltpu.VMEM((D_MODEL, V_DIM), BF16),
            pltpu.VMEM((D_MODEL, V_DIM), BF16),
            pltpu.VMEM((D_MODEL, 2 * D_MODEL), BF16),
            pltpu.VMEM((D_MODEL, LANES), BF16),
            pltpu.VMEM((W_ROWS, W_KBLK), F32),
            pltpu.VMEM((2 * N_HEADS, D_MODEL), F32),
            pltpu.SemaphoreType.DMA(()),
        ],
        compiler_params=pltpu.CompilerParams(
            dimension_semantics=("arbitrary",),
            vmem_limit_bytes=VMEM_LIMIT_BYTES,
        ),
        name="hybrid_block",
    )(x, x, p[0], wt, *consts)
```

```python
import functools

import jax
import jax.numpy as jnp
from jax import lax
from jax.experimental import pallas as pl
from jax.experimental.pallas import tpu as pltpu

D_MODEL = 1024
PLE_DIM = 256
CONV_WIDTH = 1024
CONV_K = 3
N_HEADS = 4
QK_DIM = 1024
V_DIM = 2048
DK = QK_DIM // N_HEADS
DV = V_DIM // N_HEADS
EPS = 1e-6

LANES = 128
SUBLANES = 8
TILE = 256
DVX = DV + LANES
CONV_BLK = 512
W_ROWS = 1024
W_KBLK = 256
VMEM_LIMIT_BYTES = 61 * 1024 * 1024

BF16 = jnp.bfloat16
F32 = jnp.float32


def _dot(a, b):
    return jnp.dot(a, b, preferred_element_type=F32)


def _sigmoid(v):
    return 0.5 * jnp.tanh(0.5 * v) + 0.5


def _silu(v):
    return v * _sigmoid(v)


def _rms(v, g):
    return v * lax.rsqrt(jnp.mean(v * v, axis=-1, keepdims=True) + EPS) * g


def _load_weights(wt_hbm, waout_hbm, wbout_hbm, wo_hbm, wpg_hbm, wple_hbm,
                  wconv_ref, wq_ref, wkt_ref, wv_ref, wog_ref, wzb_ref, wgab_ref, wgates_ref,
                  waout_ref, wbout_ref, wo_ref, wpg_ref, wple_ref, stage_ref, gstage_ref, sem):
    def load_transposed(row0, dst_ref, col0, scale):
        n = D_MODEL // W_KBLK

        def copy(j, slot):
            k0 = pl.multiple_of(j * W_KBLK, W_KBLK)
            return pltpu.make_async_copy(wt_hbm.at[pl.ds(row0, W_ROWS), pl.ds(k0, W_KBLK)],
                                         stage_ref.at[slot], sem.at[slot])

        copy(0, 0).start()

        def body(j, carry):
            slot = lax.rem(j, 2)

            @pl.when(j + 1 < n)
            def _():
                copy(j + 1, 1 - slot).start()

            copy(j, slot).wait()
            blk = stage_ref[slot].T
            if scale is not None:
                blk = blk * scale
            k0 = pl.multiple_of(j * W_KBLK, W_KBLK)
            dst_ref[pl.ds(k0, W_KBLK), col0:col0 + W_ROWS] = blk.astype(BF16)
            return carry

        lax.fori_loop(0, n, body, 0)

    def load_plain(src_hbm, dst_ref):
        rows, cols = dst_ref.shape
        pieces = [(r, c) for r in range(0, rows, W_ROWS) for c in range(0, cols, W_KBLK)]
        nr = min(rows, W_ROWS)

        def copy(i):
            r, c = pieces[i]
            return pltpu.make_async_copy(src_hbm.at[pl.ds(r, nr), pl.ds(c, W_KBLK)],
                                         stage_ref.at[i % 2, pl.ds(0, nr), :], sem.at[i % 2])

        copy(0).start()
        for i, (r, c) in enumerate(pieces):
            if i + 1 < len(pieces):
                copy(i + 1).start()
            copy(i).wait()
            dst_ref[r:r + nr, c:c + W_KBLK] = stage_ref[i % 2, 0:nr, :].astype(BF16)

    o = 0
    for g in range(4 * CONV_WIDTH // W_ROWS):
        load_transposed(o + g * W_ROWS, wconv_ref, g * W_ROWS, None)
    o += 4 * CONV_WIDTH
    load_transposed(o, wq_ref, 0, DK ** -0.5)
    o += QK_DIM
    load_plain(wt_hbm.at[pl.ds(o, QK_DIM), :], wkt_ref)
    o += QK_DIM
    for dst_ref in (wv_ref, wog_ref, wzb_ref):
        for g in range(V_DIM // W_ROWS):
            load_transposed(o + g * W_ROWS, dst_ref, g * W_ROWS, None)
        o += V_DIM
    gcopy = pltpu.make_async_copy(wt_hbm.at[pl.ds(o, 2 * N_HEADS), :], gstage_ref, sem.at[0])
    gcopy.start()
    gcopy.wait()
    gpad = jnp.concatenate([gstage_ref[...], jnp.zeros((LANES - 2 * N_HEADS, D_MODEL), F32)], axis=0)
    wgates_ref[...] = gpad.T.astype(BF16)
    o += 2 * N_HEADS
    for g in range(2 * D_MODEL // W_ROWS):
        load_transposed(o + g * W_ROWS, wgab_ref, g * W_ROWS, None)
    for src_hbm, dst_ref in ((waout_hbm, waout_ref), (wbout_hbm, wbout_ref), (wo_hbm, wo_ref),
                             (wpg_hbm, wpg_ref), (wple_hbm, wple_ref)):
        load_plain(src_hbm, dst_ref)


def _block_kernel(tiles_per_row,
                  x_ref, xp_ref, pp_ref, wt_hbm, waout_hbm, wbout_hbm, wo_hbm, wpg_hbm, wple_hbm,
                  gmix_ref, bgates_ref, convw_ref, convb_ref, ghead_ref, gple_ref, gfinal_ref,
                  out_ref,
                  c_ref, m_ref, u_ref, hn_ref, hnp_ref, ya_ref, yb_ref,
                  wconv_ref, wq_ref, wkt_ref, wv_ref, wog_ref, wzb_ref, wgab_ref, wgates_ref,
                  waout_ref, wbout_ref, wo_ref, wpg_ref, wple_ref,
                  stage_ref, gstage_ref, dma_sem):
    step = pl.program_id(0)
    cur = jnp.minimum(step, pl.num_programs(0) - 2)

    @pl.when(step == 0)
    def _():
        for ref in (ya_ref, yb_ref, hnp_ref):
            ref[...] = jnp.zeros_like(ref)
        _load_weights(wt_hbm, waout_hbm, wbout_hbm, wo_hbm, wpg_hbm, wple_hbm,
                      wconv_ref, wq_ref, wkt_ref, wv_ref, wog_ref, wzb_ref, wgab_ref, wgates_ref,
                      waout_ref, wbout_ref, wo_ref, wpg_ref, wple_ref, stage_ref, gstage_ref, dma_sem)

    @pl.when(lax.rem(cur, tiles_per_row) == 0)
    def _():
        c_ref[...] = jnp.zeros_like(c_ref)
        m_ref[...] = jnp.full_like(m_ref, -jnp.inf)
        u_ref[0:SUBLANES, :] = jnp.zeros((SUBLANES, CONV_WIDTH), F32)

    def hn():
        return hn_ref[...]

    def conv_in(cb):
        c0 = cb * CONV_BLK
        cs = slice(c0, c0 + CONV_BLK)
        xa = _dot(hn(), wconv_ref[:, 0 * CONV_WIDTH + c0:0 * CONV_WIDTH + c0 + CONV_BLK])
        ca = _dot(hn(), wconv_ref[:, 2 * CONV_WIDTH + c0:2 * CONV_WIDTH + c0 + CONV_BLK])
        u_ref[SUBLANES:SUBLANES + TILE, cs] = ca * xa
        u0 = u_ref[SUBLANES:SUBLANES + TILE, cs]
        u1 = u_ref[SUBLANES - 1:SUBLANES - 1 + TILE, cs]
        u2 = u_ref[SUBLANES - 2:SUBLANES - 2 + TILE, cs]
        conv = (u2 * convw_ref[0:1, cs] + u1 * convw_ref[1:2, cs] + u0 * convw_ref[2:3, cs]
                + convb_ref[:, cs])
        u_ref[0:SUBLANES, cs] = u_ref[TILE:TILE + SUBLANES, cs]
        return conv

    def conv_gate(cb, conv):
        c0 = cb * CONV_BLK
        ba = _dot(hn(), wconv_ref[:, 1 * CONV_WIDTH + c0:1 * CONV_WIDTH + c0 + CONV_BLK])
        za = _dot(hn(), wconv_ref[:, 3 * CONV_WIDTH + c0:3 * CONV_WIDTH + c0 + CONV_BLK])
        return (ba * conv * _silu(za)).astype(BF16)

    def conv_out(cb, ya_blk):
        contrib = _dot(ya_blk, waout_ref[cb * CONV_BLK:(cb + 1) * CONV_BLK, :])
        if cb == 0:
            ya_ref[...] = contrib
        else:
            ya_ref[...] += contrib

    row_i = lax.broadcasted_iota(jnp.int32, (TILE, TILE), 0)
    col_i = lax.broadcasted_iota(jnp.int32, (TILE, TILE), 1)
    causal = row_i >= col_i

    def gate_proj():
        gates = _dot(hn(), wgates_ref[...]) + bgates_ref[...]
        lf = jnp.minimum(gates, 0.0) - jnp.log1p(jnp.exp(-jnp.abs(gates)))
        lf_hi = lf.astype(BF16)
        lf_lo = (lf - lf_hi.astype(F32)).astype(BF16)
        return gates, lf_hi, lf_lo

    def gate_cumsum(gates, lf_hi, lf_lo):
        tri = causal.astype(BF16)
        bcum = _dot(tri, lf_hi) + _dot(tri, lf_lo)
        lane = lax.broadcasted_iota(jnp.int32, (TILE, LANES), 1)
        gcol = jnp.where(lane < N_HEADS, gates, bcum)
        return gcol, gcol.T

    def head_decay(h, gcol, grow):
        li_row = grow[h:h + 1, :]
        b_row = grow[N_HEADS + h:N_HEADS + h + 1, :]
        b_col = gcol[:, N_HEADS + h:N_HEADS + h + 1]
        m_st = m_ref[h, 0:1, 0:1]
        dmat = jnp.where(causal, b_col - b_row + li_row, -jnp.inf)
        a = b_col + m_st
        m_row = jnp.maximum(a, jnp.max(dmat, axis=1, keepdims=True))
        b_last = b_row[:, TILE - 1:TILE]
        g_row = b_last - b_row + li_row
        m_new = jnp.maximum(b_last + m_st, jnp.max(g_row, axis=1, keepdims=True))
        m_ref[h] = jnp.broadcast_to(m_new, (SUBLANES, LANES))
        return dict(
            dexp=jnp.exp(dmat - m_row),
            inter=jnp.exp(a - m_row),
            floor=jnp.exp(-m_row),
            w_row=jnp.exp(g_row - m_new),
            decay=jnp.exp(b_last + m_st - m_new),
        )

    def head_proj(h):
        q = _dot(hn(), wq_ref[:, h * DK:(h + 1) * DK])
        kt = lax.dot_general(wkt_ref[h * DK:(h + 1) * DK, :], hn(),
                             (((1,), (1,)), ((), ())), preferred_element_type=F32)
        v = _dot(hn(), wv_ref[:, h * DV:(h + 1) * DV]).astype(BF16)
        vx = jnp.concatenate([v, jnp.ones((TILE, LANES), BF16)], axis=1)
        return q, kt, vx

    def head_scores(q, kt):
        return _dot(q.astype(BF16), kt.astype(BF16))

    def head_num(h, q, vx, s, dk):
        pmat = (s * dk["dexp"]).astype(BF16)
        qs = (q * dk["inter"]).astype(BF16)
        numx = _dot(jnp.concatenate([pmat, qs], axis=1),
                    jnp.concatenate([vx, c_ref[h].astype(BF16)], axis=0))
        den = numx[:, DV:DV + 1]
        return numx[:, 0:DV] / jnp.maximum(jnp.abs(den), dk["floor"])

    def head_ktw(kt, dk):
        return (kt * dk["w_row"]).astype(BF16)

    def head_upd(h, ktw, vx, dk):
        c_ref[h] = dk["decay"] * c_ref[h] + _dot(ktw, vx)

    def head_gate(h):
        og = _dot(hn(), wog_ref[:, h * DV:(h + 1) * DV])
        zb = _dot(hn(), wzb_ref[:, h * DV:(h + 1) * DV])
        return _sigmoid(og) * _silu(zb)

    def head_out(h, hh, gate):
        yb_blk = (_rms(hh, ghead_ref[:, h * DV:(h + 1) * DV]) * gate).astype(BF16)
        contrib = _dot(yb_blk, wbout_ref[h * DV:(h + 1) * DV, :])
        if h == 0:
            yb_ref[...] = contrib
        else:
            yb_ref[...] += contrib

    hnp = hnp_ref[...]
    ga = _dot(hnp, wgab_ref[:, 0:D_MODEL])
    gb = _dot(hnp, wgab_ref[:, D_MODEL:2 * D_MODEL])
    pe = _dot(pp_ref[...].astype(BF16), wple_ref[...])
    hn_ref[...] = _rms(x_ref[...], gmix_ref[...]).astype(BF16)
    gparts = gate_proj()
    conv0 = conv_in(0)
    gcol, grow = gate_cumsum(*gparts)
    merged = (_sigmoid(ga) * ya_ref[...] + _sigmoid(gb) * yb_ref[...]).astype(BF16)
    ya0 = conv_gate(0, conv0)
    dk0 = head_decay(0, gcol, grow)
    x1 = xp_ref[...] + _dot(merged, wo_ref[...])
    r1 = _rms(x1, gple_ref[...]).astype(BF16)
    conv1 = conv_in(1)
    conv_out(0, ya0)
    dk1 = head_decay(1, gcol, grow)
    ple_gate = _sigmoid(_dot(r1, wpg_ref[...]))
    out_ref[...] = _rms(x1 + ple_gate * pe, gfinal_ref[...])
    ya1 = conv_gate(1, conv1)

    q0, kt0, vx0 = head_proj(0)
    conv_out(1, ya1)
    s0 = head_scores(q0, kt0)
    q1, kt1, vx1 = head_proj(1)
    hh0 = head_num(0, q0, vx0, s0, dk0)
    ktw0 = head_ktw(kt0, dk0)
    s1 = head_scores(q1, kt1)
    dk2 = head_decay(2, gcol, grow)
    gate0 = head_gate(0)
    q2, kt2, vx2 = head_proj(2)
    head_out(0, hh0, gate0)
    hh1 = head_num(1, q1, vx1, s1, dk1)
    ktw1 = head_ktw(kt1, dk1)
    s2 = head_scores(q2, kt2)
    dk3 = head_decay(3, gcol, grow)
    gate1 = head_gate(1)
    q3, kt3, vx3 = head_proj(3)
    head_out(1, hh1, gate1)
    hh2 = head_num(2, q2, vx2, s2, dk2)
    ktw2 = head_ktw(kt2, dk2)
    s3 = head_scores(q3, kt3)
    gate2 = head_gate(2)
    head_out(2, hh2, gate2)
    hh3 = head_num(3, q3, vx3, s3, dk3)
    ktw3 = head_ktw(kt3, dk3)
    gate3 = head_gate(3)
    head_upd(0, ktw0, vx0, dk0)
    head_upd(1, ktw1, vx1, dk1)
    head_upd(2, ktw2, vx2, dk2)
    head_out(3, hh3, gate3)
    head_upd(3, ktw3, vx3, dk3)
    hnp_ref[...] = hn()


def _const_spec(shape):
    return pl.BlockSpec(shape, lambda s: (0,) * len(shape), pipeline_mode=pl.Buffered(1))


def kernel(x, p, g_mix, w_in, conv_w, conv_b, w_a_out, b_gates, g_head, w_b_out, w_o, g_ple,
           w_ple_gate, w_ple, g_final):
    bsz, seq, _ = x.shape
    assert w_in.shape[0] == 1 and seq % TILE == 0
    tiles_per_row = seq // TILE
    n_tiles = bsz * tiles_per_row
    assert w_in.shape[2] == 4 * CONV_WIDTH + 2 * QK_DIM + 3 * V_DIM + 2 * N_HEADS + 2 * D_MODEL
    wt = jnp.transpose(w_in[0])
    b_g = jnp.pad(b_gates[0], (0, LANES - 2 * N_HEADS)).reshape(1, LANES)

    hbm_weights = (wt, w_a_out[0], w_b_out[0], w_o[0], w_ple_gate[0], w_ple[0])
    consts = (
        g_mix[0].reshape(1, D_MODEL), b_g, conv_w[0], conv_b[0].reshape(1, CONV_WIDTH),
        g_head[0].reshape(1, V_DIM), g_ple[0].reshape(1, D_MODEL), g_final.reshape(1, D_MODEL),
    )

    def cur_map(s):
        c = jnp.minimum(s, n_tiles - 1)
        return (c // tiles_per_row, c % tiles_per_row, 0)

    def prev_map(s):
        c = jnp.maximum(s - 1, 0)
        return (c // tiles_per_row, c % tiles_per_row, 0)

    in_specs = [
        pl.BlockSpec((None, TILE, D_MODEL), cur_map),
        pl.BlockSpec((None, TILE, D_MODEL), prev_map),
        pl.BlockSpec((None, TILE, PLE_DIM), prev_map),
    ] + [pl.BlockSpec(memory_space=pl.ANY)] * len(hbm_weights) + [_const_spec(a.shape) for a in consts]

    return pl.pallas_call(
        functools.partial(_block_kernel, tiles_per_row),
        grid=(n_tiles + 1,),
        in_specs=in_specs,
        out_specs=pl.BlockSpec((None, TILE, D_MODEL), prev_map),
        out_shape=jax.ShapeDtypeStruct(x.shape, x.dtype),
        scratch_shapes=[
            pltpu.VMEM((N_HEADS, DK, DVX), F32),
            pltpu.VMEM((N_HEADS, SUBLANES, LANES), F32),
            pltpu.VMEM((TILE + SUBLANES, CONV_WIDTH), F32),
            pltpu.VMEM((TILE, D_MODEL), BF16),
            pltpu.VMEM((TILE, D_MODEL), BF16),
            pltpu.VMEM((TILE, D_MODEL), F32),
            pltpu.VMEM((TILE, D_MODEL), F32),
            pltpu.VMEM((D_MODEL, 4 * CONV_WIDTH), BF16),
            pltpu.VMEM((D_MODEL, QK_DIM), BF16),
            pltpu.VMEM((QK_DIM, D_MODEL), BF16),
            pltpu.VMEM((D_MODEL, V_DIM), BF16),
            pltpu.VMEM((D_MODEL, V_DIM), BF16),
            pltpu.VMEM((D_MODEL, V_DIM), BF16),
            pltpu.VMEM((D_MODEL, 2 * D_MODEL), BF16),
            pltpu.VMEM((D_MODEL, LANES), BF16),
            pltpu.VMEM((CONV_WIDTH, D_MODEL), BF16),
            pltpu.VMEM((V_DIM, D_MODEL), BF16),
            pltpu.VMEM((D_MODEL, D_MODEL), BF16),
            pltpu.VMEM((D_MODEL, D_MODEL), BF16),
            pltpu.VMEM((PLE_DIM, D_MODEL), BF16),
            pltpu.VMEM((2, W_ROWS, W_KBLK), F32),
            pltpu.VMEM((2 * N_HEADS, D_MODEL), F32),
            pltpu.SemaphoreType.DMA((2,)),
        ],
        compiler_params=pltpu.CompilerParams(
            dimension_semantics=("arbitrary",),
            vmem_limit_bytes=VMEM_LIMIT_BYTES,
        ),
        name="hybrid_block",
    )(x, x, p[0], *hbm_weights, *consts)
```

```python
import functools

import jax
import jax.numpy as jnp
from jax import lax
from jax.experimental import pallas as pl
from jax.experimental.pallas import tpu as pltpu

D_MODEL = 1024
PLE_DIM = 256
CONV_WIDTH = 1024
CONV_K = 3
N_HEADS = 4
QK_DIM = 1024
V_DIM = 2048
DK = QK_DIM // N_HEADS
DV = V_DIM // N_HEADS
EPS = 1e-6

LANES = 128
SUBLANES = 8
TILE = 256
DVX = DV + LANES
CONV_BLK = 512
W_ROWS = 1024
W_KBLK = 256
W_SPLIT = 4
VMEM_LIMIT_BYTES = 61 * 1024 * 1024

BF16 = jnp.bfloat16
F32 = jnp.float32


def _dot(a, b):
    return jnp.dot(a, b, preferred_element_type=F32)


def _sigmoid(v):
    return 0.5 * jnp.tanh(0.5 * v) + 0.5


def _silu(v):
    return v * _sigmoid(v)


def _rms(v, g):
    return v * lax.rsqrt(jnp.mean(v * v, axis=-1, keepdims=True) + EPS) * g


def _load_weights(wt_hbm, waout_hbm, wbout_hbm, wo_hbm, wpg_hbm, wple_hbm,
                  wconv_ref, wq_ref, wkt_ref, wv_ref, wog_ref, wzb_ref, wgab_ref, wgates_ref,
                  waout_ref, wbout_ref, wo_ref, wpg_ref, wple_ref, stage_ref, gstage_ref, sem):
    def fetches(src_hbm, row0, nr, col0, slot):
        band = nr // W_SPLIT
        return [pltpu.make_async_copy(src_hbm.at[pl.ds(row0 + b * band, band), pl.ds(col0, W_KBLK)],
                                      stage_ref.at[slot, pl.ds(b * band, band), :], sem.at[slot, b])
                for b in range(W_SPLIT)]

    def start(copies):
        for b, cp in enumerate(copies):
            cp.start(priority=b % 2)

    def wait(copies):
        for cp in copies:
            cp.wait()

    def load_transposed(row0, dst_ref, col0, scale):
        n = D_MODEL // W_KBLK

        def copies(j, slot):
            return fetches(wt_hbm, row0, W_ROWS, pl.multiple_of(j * W_KBLK, W_KBLK), slot)

        start(copies(0, 0))

        def body(j, carry):
            slot = lax.rem(j, 2)

            @pl.when(j + 1 < n)
            def _():
                start(copies(j + 1, 1 - slot))

            wait(copies(j, slot))
            blk = stage_ref[slot].T
            if scale is not None:
                blk = blk * scale
            k0 = pl.multiple_of(j * W_KBLK, W_KBLK)
            dst_ref[pl.ds(k0, W_KBLK), col0:col0 + W_ROWS] = blk.astype(BF16)
            return carry

        lax.fori_loop(0, n, body, 0)

    def load_plain(src_hbm, dst_ref):
        rows, cols = dst_ref.shape
        pieces = [(r, c) for r in range(0, rows, W_ROWS) for c in range(0, cols, W_KBLK)]
        nr = min(rows, W_ROWS)

        def copies(i):
            r, c = pieces[i]
            return fetches(src_hbm, r, nr, c, i % 2)

        start(copies(0))
        for i, (r, c) in enumerate(pieces):
            if i + 1 < len(pieces):
                start(copies(i + 1))
            wait(copies(i))
            dst_ref[r:r + nr, c:c + W_KBLK] = stage_ref[i % 2, 0:nr, :].astype(BF16)

    o = 0
    for g in range(4 * CONV_WIDTH // W_ROWS):
        load_transposed(o + g * W_ROWS, wconv_ref, g * W_ROWS, None)
    o += 4 * CONV_WIDTH
    load_transposed(o, wq_ref, 0, DK ** -0.5)
    o += QK_DIM
    load_plain(wt_hbm.at[pl.ds(o, QK_DIM), :], wkt_ref)
    o += QK_DIM
    for dst_ref in (wv_ref, wog_ref, wzb_ref):
        for g in range(V_DIM // W_ROWS):
            load_transposed(o + g * W_ROWS, dst_ref, g * W_ROWS, None)
        o += V_DIM
    gcopy = pltpu.make_async_copy(wt_hbm.at[pl.ds(o, 2 * N_HEADS), :], gstage_ref, sem.at[0, 0])
    gcopy.start()
    gcopy.wait()
    gpad = jnp.concatenate([gstage_ref[...], jnp.zeros((LANES - 2 * N_HEADS, D_MODEL), F32)], axis=0)
    wgates_ref[...] = gpad.T.astype(BF16)
    o += 2 * N_HEADS
    for g in range(2 * D_MODEL // W_ROWS):
        load_transposed(o + g * W_ROWS, wgab_ref, g * W_ROWS, None)
    for src_hbm, dst_ref in ((waout_hbm, waout_ref), (wbout_hbm, wbout_ref), (wo_hbm, wo_ref),
                             (wpg_hbm, wpg_ref), (wple_hbm, wple_ref)):
        load_plain(src_hbm, dst_ref)


def _block_kernel(tiles_per_row,
                  x_ref, xp_ref, pp_ref, wt_hbm, waout_hbm, wbout_hbm, wo_hbm, wpg_hbm, wple_hbm,
                  gmix_ref, bgates_ref, convw_ref, convb_ref, ghead_ref, gple_ref, gfinal_ref,
                  out_ref,
                  c_ref, m_ref, u_ref, hn_ref, hnp_ref, ya_ref, yb_ref,
                  wconv_ref, wq_ref, wkt_ref, wv_ref, wog_ref, wzb_ref, wgab_ref, wgates_ref,
                  waout_ref, wbout_ref, wo_ref, wpg_ref, wple_ref,
                  stage_ref, gstage_ref, dma_sem):
    step = pl.program_id(0)
    cur = jnp.minimum(step, pl.num_programs(0) - 2)

    @pl.when(step == 0)
    def _():
        for ref in (ya_ref, yb_ref, hnp_ref):
            ref[...] = jnp.zeros_like(ref)
        _load_weights(wt_hbm, waout_hbm, wbout_hbm, wo_hbm, wpg_hbm, wple_hbm,
                      wconv_ref, wq_ref, wkt_ref, wv_ref, wog_ref, wzb_ref, wgab_ref, wgates_ref,
                      waout_ref, wbout_ref, wo_ref, wpg_ref, wple_ref, stage_ref, gstage_ref, dma_sem)

    @pl.when(lax.rem(cur, tiles_per_row) == 0)
    def _():
        c_ref[...] = jnp.zeros_like(c_ref)
        m_ref[...] = jnp.full_like(m_ref, -jnp.inf)
        u_ref[0:SUBLANES, :] = jnp.zeros((SUBLANES, CONV_WIDTH), F32)

    def hn():
        return hn_ref[...]

    def conv_in(cb):
        c0 = cb * CONV_BLK
        cs = slice(c0, c0 + CONV_BLK)
        xa = _dot(hn(), wconv_ref[:, 0 * CONV_WIDTH + c0:0 * CONV_WIDTH + c0 + CONV_BLK])
        ca = _dot(hn(), wconv_ref[:, 2 * CONV_WIDTH + c0:2 * CONV_WIDTH + c0 + CONV_BLK])
        u_ref[SUBLANES:SUBLANES + TILE, cs] = ca * xa
        u0 = u_ref[SUBLANES:SUBLANES + TILE, cs]
        u1 = u_ref[SUBLANES - 1:SUBLANES - 1 + TILE, cs]
        u2 = u_ref[SUBLANES - 2:SUBLANES - 2 + TILE, cs]
        conv = (u2 * convw_ref[0:1, cs] + u1 * convw_ref[1:2, cs] + u0 * convw_ref[2:3, cs]
                + convb_ref[:, cs])
        u_ref[0:SUBLANES, cs] = u_ref[TILE:TILE + SUBLANES, cs]
        return conv

    def conv_gate(cb, conv):
        c0 = cb * CONV_BLK
        ba = _dot(hn(), wconv_ref[:, 1 * CONV_WIDTH + c0:1 * CONV_WIDTH + c0 + CONV_BLK])
        za = _dot(hn(), wconv_ref[:, 3 * CONV_WIDTH + c0:3 * CONV_WIDTH + c0 + CONV_BLK])
        return (ba * conv * _silu(za)).astype(BF16)

    def conv_out(cb, ya_blk):
        contrib = _dot(ya_blk, waout_ref[cb * CONV_BLK:(cb + 1) * CONV_BLK, :])
        if cb == 0:
            ya_ref[...] = contrib
        else:
            ya_ref[...] += contrib

    row_i = lax.broadcasted_iota(jnp.int32, (TILE, TILE), 0)
    col_i = lax.broadcasted_iota(jnp.int32, (TILE, TILE), 1)
    causal = row_i >= col_i

    def gate_proj():
        gates = _dot(hn(), wgates_ref[...]) + bgates_ref[...]
        lf = jnp.minimum(gates, 0.0) - jnp.log1p(jnp.exp(-jnp.abs(gates)))
        lf_hi = lf.astype(BF16)
        lf_lo = (lf - lf_hi.astype(F32)).astype(BF16)
        return gates, lf_hi, lf_lo

    def gate_cumsum(gates, lf_hi, lf_lo):
        tri = causal.astype(BF16)
        bcum = _dot(tri, lf_hi) + _dot(tri, lf_lo)
        lane = lax.broadcasted_iota(jnp.int32, (TILE, LANES), 1)
        gcol = jnp.where(lane < N_HEADS, gates, bcum)
        return gcol, gcol.T

    def head_decay(h, gcol, grow):
        li_row = grow[h:h + 1, :]
        b_row = grow[N_HEADS + h:N_HEADS + h + 1, :]
        b_col = gcol[:, N_HEADS + h:N_HEADS + h + 1]
        m_st = m_ref[h, 0:1, 0:1]
        dmat = jnp.where(causal, b_col - b_row + li_row, -jnp.inf)
        a = b_col + m_st
        m_row = jnp.maximum(a, jnp.max(dmat, axis=1, keepdims=True))
        b_last = b_row[:, TILE - 1:TILE]
        g_row = b_last - b_row + li_row
        m_new = jnp.maximum(b_last + m_st, jnp.max(g_row, axis=1, keepdims=True))
        m_ref[h] = jnp.broadcast_to(m_new, (SUBLANES, LANES))
        return dict(
            dexp=jnp.exp(dmat - m_row),
            inter=jnp.exp(a - m_row),
            floor=jnp.exp(-m_row),
            w_row=jnp.exp(g_row - m_new),
            decay=jnp.exp(b_last + m_st - m_new),
        )

    def head_proj(h):
        q = _dot(hn(), wq_ref[:, h * DK:(h + 1) * DK])
        kt = lax.dot_general(wkt_ref[h * DK:(h + 1) * DK, :], hn(),
                             (((1,), (1,)), ((), ())), preferred_element_type=F32)
        v = _dot(hn(), wv_ref[:, h * DV:(h + 1) * DV]).astype(BF16)
        vx = jnp.concatenate([v, jnp.ones((TILE, LANES), BF16)], axis=1)
        return q, kt, vx

    def head_scores(q, kt):
        return _dot(q.astype(BF16), kt.astype(BF16))

    def head_num(h, q, vx, s, dk):
        pmat = (s * dk["dexp"]).astype(BF16)
        qs = (q * dk["inter"]).astype(BF16)
        numx = _dot(jnp.concatenate([pmat, qs], axis=1),
                    jnp.concatenate([vx, c_ref[h].astype(BF16)], axis=0))
        den = numx[:, DV:DV + 1]
        return numx[:, 0:DV] / jnp.maximum(jnp.abs(den), dk["floor"])

    def head_ktw(kt, dk):
        return (kt * dk["w_row"]).astype(BF16)

    def head_upd(h, ktw, vx, dk):
        c_ref[h] = dk["decay"] * c_ref[h] + _dot(ktw, vx)

    def head_gate(h):
        og = _dot(hn(), wog_ref[:, h * DV:(h + 1) * DV])
        zb = _dot(hn(), wzb_ref[:, h * DV:(h + 1) * DV])
        return _sigmoid(og) * _silu(zb)

    def head_out(h, hh, gate):
        yb_blk = (_rms(hh, ghead_ref[:, h * DV:(h + 1) * DV]) * gate).astype(BF16)
        contrib = _dot(yb_blk, wbout_ref[h * DV:(h + 1) * DV, :])
        if h == 0:
            yb_ref[...] = contrib
        else:
            yb_ref[...] += contrib

    hnp = hnp_ref[...]
    ga = _dot(hnp, wgab_ref[:, 0:D_MODEL])
    gb = _dot(hnp, wgab_ref[:, D_MODEL:2 * D_MODEL])
    pe = _dot(pp_ref[...].astype(BF16), wple_ref[...])
    hn_ref[...] = _rms(x_ref[...], gmix_ref[...]).astype(BF16)
    gparts = gate_proj()
    conv0 = conv_in(0)
    gcol, grow = gate_cumsum(*gparts)
    merged = (_sigmoid(ga) * ya_ref[...] + _sigmoid(gb) * yb_ref[...]).astype(BF16)
    ya0 = conv_gate(0, conv0)
    dk0 = head_decay(0, gcol, grow)
    x1 = xp_ref[...] + _dot(merged, wo_ref[...])
    r1 = _rms(x1, gple_ref[...]).astype(BF16)
    conv1 = conv_in(1)
    conv_out(0, ya0)
    dk1 = head_decay(1, gcol, grow)
    ple_gate = _sigmoid(_dot(r1, wpg_ref[...]))
    out_ref[...] = _rms(x1 + ple_gate * pe, gfinal_ref[...])
    ya1 = conv_gate(1, conv1)

    q0, kt0, vx0 = head_proj(0)
    conv_out(1, ya1)
    s0 = head_scores(q0, kt0)
    q1, kt1, vx1 = head_proj(1)
    hh0 = head_num(0, q0, vx0, s0, dk0)
    ktw0 = head_ktw(kt0, dk0)
    s1 = head_scores(q1, kt1)
    dk2 = head_decay(2, gcol, grow)
    gate0 = head_gate(0)
    q2, kt2, vx2 = head_proj(2)
    head_out(0, hh0, gate0)
    hh1 = head_num(1, q1, vx1, s1, dk1)
    ktw1 = head_ktw(kt1, dk1)
    s2 = head_scores(q2, kt2)
    dk3 = head_decay(3, gcol, grow)
    gate1 = head_gate(1)
    q3, kt3, vx3 = head_proj(3)
    head_out(1, hh1, gate1)
    hh2 = head_num(2, q2, vx2, s2, dk2)
    ktw2 = head_ktw(kt2, dk2)
    s3 = head_scores(q3, kt3)
    gate2 = head_gate(2)
    head_out(2, hh2, gate2)
    hh3 = head_num(3, q3, vx3, s3, dk3)
    ktw3 = head_ktw(kt3, dk3)
    gate3 = head_gate(3)
    head_upd(0, ktw0, vx0, dk0)
    head_upd(1, ktw1, vx1, dk1)
    head_upd(2, ktw2, vx2, dk2)
    head_out(3, hh3, gate3)
    head_upd(3, ktw3, vx3, dk3)
    hnp_ref[...] = hn()


def _const_spec(shape):
    return pl.BlockSpec(shape, lambda s: (0,) * len(shape), pipeline_mode=pl.Buffered(1))


def kernel(x, p, g_mix, w_in, conv_w, conv_b, w_a_out, b_gates, g_head, w_b_out, w_o, g_ple,
           w_ple_gate, w_ple, g_final):
    bsz, seq, _ = x.shape
    assert w_in.shape[0] == 1 and seq % TILE == 0
    tiles_per_row = seq // TILE
    n_tiles = bsz * tiles_per_row
    assert w_in.shape[2] == 4 * CONV_WIDTH + 2 * QK_DIM + 3 * V_DIM + 2 * N_HEADS + 2 * D_MODEL
    wt = jnp.transpose(w_in[0])
    b_g = jnp.pad(b_gates[0], (0, LANES - 2 * N_HEADS)).reshape(1, LANES)

    hbm_weights = (wt, w_a_out[0], w_b_out[0], w_o[0], w_ple_gate[0], w_ple[0])
    consts = (
        g_mix[0].reshape(1, D_MODEL), b_g, conv_w[0], conv_b[0].reshape(1, CONV_WIDTH),
        g_head[0].reshape(1, V_DIM), g_ple[0].reshape(1, D_MODEL), g_final.reshape(1, D_MODEL),
    )

    def cur_map(s):
        c = jnp.minimum(s, n_tiles - 1)
        return (c // tiles_per_row, c % tiles_per_row, 0)

    def prev_map(s):
        c = jnp.maximum(s - 1, 0)
        return (c // tiles_per_row, c % tiles_per_row, 0)

    in_specs = [
        pl.BlockSpec((None, TILE, D_MODEL), cur_map),
        pl.BlockSpec((None, TILE, D_MODEL), prev_map),
        pl.BlockSpec((None, TILE, PLE_DIM), prev_map),
    ] + [pl.BlockSpec(memory_space=pl.ANY)] * len(hbm_weights) + [_const_spec(a.shape) for a in consts]

    return pl.pallas_call(
        functools.partial(_block_kernel, tiles_per_row),
        grid=(n_tiles + 1,),
        in_specs=in_specs,
        out_specs=pl.BlockSpec((None, TILE, D_MODEL), prev_map),
        out_shape=jax.ShapeDtypeStruct(x.shape, x.dtype),
        scratch_shapes=[
            pltpu.VMEM((N_HEADS, DK, DVX), F32),
            pltpu.VMEM((N_HEADS, SUBLANES, LANES), F32),
            pltpu.VMEM((TILE + SUBLANES, CONV_WIDTH), F32),
            pltpu.VMEM((TILE, D_MODEL), BF16),
            pltpu.VMEM((TILE, D_MODEL), BF16),
            pltpu.VMEM((TILE, D_MODEL), F32),
            pltpu.VMEM((TILE, D_MODEL), F32),
            pltpu.VMEM((D_MODEL, 4 * CONV_WIDTH), BF16),
            pltpu.VMEM((D_MODEL, QK_DIM), BF16),
            pltpu.VMEM((QK_DIM, D_MODEL), BF16),
            pltpu.VMEM((D_MODEL, V_DIM), BF16),
            pltpu.VMEM((D_MODEL, V_DIM), BF16),
            pltpu.VMEM((D_MODEL, V_DIM), BF16),
            pltpu.VMEM((D_MODEL, 2 * D_MODEL), BF16),
            pltpu.VMEM((D_MODEL, LANES), BF16),
            pltpu.VMEM((CONV_WIDTH, D_MODEL), BF16),
            pltpu.VMEM((V_DIM, D_MODEL), BF16),
            pltpu.VMEM((D_MODEL, D_MODEL), BF16),
            pltpu.VMEM((D_MODEL, D_MODEL), BF16),
            pltpu.VMEM((PLE_DIM, D_MODEL), BF16),
            pltpu.VMEM((2, W_ROWS, W_KBLK), F32),
            pltpu.VMEM((2 * N_HEADS, D_MODEL), F32),
            pltpu.SemaphoreType.DMA((2, W_SPLIT)),
        ],
        compiler_params=pltpu.CompilerParams(
            dimension_semantics=("arbitrary",),
            vmem_limit_bytes=VMEM_LIMIT_BYTES,
        ),
        name="hybrid_block",
    )(x, x, p[0], *hbm_weights, *consts)
```

```python
import functools

import jax
import jax.numpy as jnp
from jax import lax
from jax.experimental import pallas as pl
from jax.experimental.pallas import tpu as pltpu

D_MODEL = 1024
PLE_DIM = 256
CONV_WIDTH = 1024
CONV_K = 3
N_HEADS = 4
QK_DIM = 1024
V_DIM = 2048
DK = QK_DIM // N_HEADS
DV = V_DIM // N_HEADS
EPS = 1e-6

LANES = 128
SUBLANES = 8
TILE = 256
DVX = DV + LANES
CONV_BLK = 512
W_ROWS = 1024
W_KBLK = 256
W_SPLIT = 4
VMEM_LIMIT_BYTES = 61 * 1024 * 1024

BF16 = jnp.bfloat16
F32 = jnp.float32


def _dot(a, b):
    return jnp.dot(a, b, preferred_element_type=F32)


def _sigmoid(v):
    return 0.5 * jnp.tanh(0.5 * v) + 0.5


def _silu(v):
    return v * _sigmoid(v)


def _rms(v, g):
    return v * lax.rsqrt(jnp.mean(v * v, axis=-1, keepdims=True) + EPS) * g


def _load_weights(wt_hbm, waout_hbm, wbout_hbm, wo_hbm, wpg_hbm, wple_hbm,
                  wconv_ref, wq_ref, wkt_ref, wv_ref, wog_ref, wzb_ref, wgab_ref, wgates_ref,
                  waout_ref, wbout_ref, wo_ref, wpg_ref, wple_ref, stage_ref, gstage_ref, sem):
    eye = (lax.broadcasted_iota(jnp.int32, (W_KBLK, W_KBLK), 0)
           == lax.broadcasted_iota(jnp.int32, (W_KBLK, W_KBLK), 1)).astype(BF16)

    def fetches(src_hbm, row0, nr, col0, slot):
        band = nr // W_SPLIT
        return [pltpu.make_async_copy(src_hbm.at[pl.ds(row0 + b * band, band), pl.ds(col0, W_KBLK)],
                                      stage_ref.at[slot, pl.ds(b * band, band), :], sem.at[slot, b])
                for b in range(W_SPLIT)]

    def start(copies):
        for b, cp in enumerate(copies):
            cp.start(priority=b % 2)

    def wait(copies):
        for cp in copies:
            cp.wait()

    def load_transposed(row0, dst_ref, col0, scale):
        n = D_MODEL // W_KBLK

        def copies(j, slot):
            return fetches(wt_hbm, row0, W_ROWS, pl.multiple_of(j * W_KBLK, W_KBLK), slot)

        start(copies(0, 0))

        def body(j, carry):
            slot = lax.rem(j, 2)

            @pl.when(j + 1 < n)
            def _():
                start(copies(j + 1, 1 - slot))

            wait(copies(j, slot))
            blk = stage_ref[slot]
            if scale is not None:
                blk = blk * scale
            blk_t = lax.dot_general(eye, blk.astype(BF16), (((1,), (1,)), ((), ())),
                                    preferred_element_type=F32)
            k0 = pl.multiple_of(j * W_KBLK, W_KBLK)
            dst_ref[pl.ds(k0, W_KBLK), col0:col0 + W_ROWS] = blk_t.astype(BF16)
            return carry

        lax.fori_loop(0, n, body, 0)

    def load_plain(src_hbm, dst_ref):
        rows, cols = dst_ref.shape
        pieces = [(r, c) for r in range(0, rows, W_ROWS) for c in range(0, cols, W_KBLK)]
        nr = min(rows, W_ROWS)

        def copies(i):
            r, c = pieces[i]
            return fetches(src_hbm, r, nr, c, i % 2)

        start(copies(0))
        for i, (r, c) in enumerate(pieces):
            if i + 1 < len(pieces):
                start(copies(i + 1))
            wait(copies(i))
            dst_ref[r:r + nr, c:c + W_KBLK] = stage_ref[i % 2, 0:nr, :].astype(BF16)

    o = 0
    for g in range(4 * CONV_WIDTH // W_ROWS):
        load_transposed(o + g * W_ROWS, wconv_ref, g * W_ROWS, None)
    o += 4 * CONV_WIDTH
    load_transposed(o, wq_ref, 0, DK ** -0.5)
    o += QK_DIM
    load_plain(wt_hbm.at[pl.ds(o, QK_DIM), :], wkt_ref)
    o += QK_DIM
    for dst_ref in (wv_ref, wog_ref, wzb_ref):
        for g in range(V_DIM // W_ROWS):
            load_transposed(o + g * W_ROWS, dst_ref, g * W_ROWS, None)
        o += V_DIM
    gcopy = pltpu.make_async_copy(wt_hbm.at[pl.ds(o, 2 * N_HEADS), :], gstage_ref, sem.at[0, 0])
    gcopy.start()
    gcopy.wait()
    gpad = jnp.concatenate([gstage_ref[...], jnp.zeros((LANES - 2 * N_HEADS, D_MODEL), F32)], axis=0)
    wgates_ref[...] = gpad.T.astype(BF16)
    o += 2 * N_HEADS
    for g in range(2 * D_MODEL // W_ROWS):
        load_transposed(o + g * W_ROWS, wgab_ref, g * W_ROWS, None)
    for src_hbm, dst_ref in ((waout_hbm, waout_ref), (wbout_hbm, wbout_ref), (wo_hbm, wo_ref),
                             (wpg_hbm, wpg_ref), (wple_hbm, wple_ref)):
        load_plain(src_hbm, dst_ref)


def _block_kernel(tiles_per_row,
                  x_ref, xp_ref, pp_ref, wt_hbm, waout_hbm, wbout_hbm, wo_hbm, wpg_hbm, wple_hbm,
                  gmix_ref, bgates_ref, convw_ref, convb_ref, ghead_ref, gple_ref, gfinal_ref,
                  out_ref,
                  c_ref, m_ref, u_ref, hn_ref, hnp_ref, ya_ref, yb_ref,
                  wconv_ref, wq_ref, wkt_ref, wv_ref, wog_ref, wzb_ref, wgab_ref, wgates_ref,
                  waout_ref, wbout_ref, wo_ref, wpg_ref, wple_ref,
                  stage_ref, gstage_ref, dma_sem):
    step = pl.program_id(0)
    cur = jnp.minimum(step, pl.num_programs(0) - 2)

    @pl.when(step == 0)
    def _():
        for ref in (ya_ref, yb_ref, hnp_ref):
            ref[...] = jnp.zeros_like(ref)
        _load_weights(wt_hbm, waout_hbm, wbout_hbm, wo_hbm, wpg_hbm, wple_hbm,
                      wconv_ref, wq_ref, wkt_ref, wv_ref, wog_ref, wzb_ref, wgab_ref, wgates_ref,
                      waout_ref, wbout_ref, wo_ref, wpg_ref, wple_ref, stage_ref, gstage_ref, dma_sem)

    @pl.when(lax.rem(cur, tiles_per_row) == 0)
    def _():
        c_ref[...] = jnp.zeros_like(c_ref)
        m_ref[...] = jnp.full_like(m_ref, -jnp.inf)
        u_ref[0:SUBLANES, :] = jnp.zeros((SUBLANES, CONV_WIDTH), F32)

    def hn():
        return hn_ref[...]

    def conv_in(cb):
        c0 = cb * CONV_BLK
        cs = slice(c0, c0 + CONV_BLK)
        xa = _dot(hn(), wconv_ref[:, 0 * CONV_WIDTH + c0:0 * CONV_WIDTH + c0 + CONV_BLK])
        ca = _dot(hn(), wconv_ref[:, 2 * CONV_WIDTH + c0:2 * CONV_WIDTH + c0 + CONV_BLK])
        u_ref[SUBLANES:SUBLANES + TILE, cs] = ca * xa
        u0 = u_ref[SUBLANES:SUBLANES + TILE, cs]
        u1 = u_ref[SUBLANES - 1:SUBLANES - 1 + TILE, cs]
        u2 = u_ref[SUBLANES - 2:SUBLANES - 2 + TILE, cs]
        conv = (u2 * convw_ref[0:1, cs] + u1 * convw_ref[1:2, cs] + u0 * convw_ref[2:3, cs]
                + convb_ref[:, cs])
        u_ref[0:SUBLANES, cs] = u_ref[TILE:TILE + SUBLANES, cs]
        return conv

    def conv_gate(cb, conv):
        c0 = cb * CONV_BLK
        ba = _dot(hn(), wconv_ref[:, 1 * CONV_WIDTH + c0:1 * CONV_WIDTH + c0 + CONV_BLK])
        za = _dot(hn(), wconv_ref[:, 3 * CONV_WIDTH + c0:3 * CONV_WIDTH + c0 + CONV_BLK])
        return (ba * conv * _silu(za)).astype(BF16)

    def conv_out(cb, ya_blk):
        contrib = _dot(ya_blk, waout_ref[cb * CONV_BLK:(cb + 1) * CONV_BLK, :])
        if cb == 0:
            ya_ref[...] = contrib
        else:
            ya_ref[...] += contrib

    row_i = lax.broadcasted_iota(jnp.int32, (TILE, TILE), 0)
    col_i = lax.broadcasted_iota(jnp.int32, (TILE, TILE), 1)
    causal = row_i >= col_i

    def gate_proj():
        gates = _dot(hn(), wgates_ref[...]) + bgates_ref[...]
        lf = jnp.minimum(gates, 0.0) - jnp.log1p(jnp.exp(-jnp.abs(gates)))
        lf_hi = lf.astype(BF16)
        lf_lo = (lf - lf_hi.astype(F32)).astype(BF16)
        return gates, lf_hi, lf_lo

    def gate_cumsum(gates, lf_hi, lf_lo):
        tri = causal.astype(BF16)
        bcum = _dot(tri, lf_hi) + _dot(tri, lf_lo)
        lane = lax.broadcasted_iota(jnp.int32, (TILE, LANES), 1)
        gcol = jnp.where(lane < N_HEADS, gates, bcum)
        return gcol, gcol.T

    def head_decay(h, gcol, grow):
        li_row = grow[h:h + 1, :]
        b_row = grow[N_HEADS + h:N_HEADS + h + 1, :]
        b_col = gcol[:, N_HEADS + h:N_HEADS + h + 1]
        m_st = m_ref[h, 0:1, 0:1]
        dmat = jnp.where(causal, b_col - b_row + li_row, -jnp.inf)
        a = b_col + m_st
        m_row = jnp.maximum(a, jnp.max(dmat, axis=1, keepdims=True))
        b_last = b_row[:, TILE - 1:TILE]
        g_row = b_last - b_row + li_row
        m_new = jnp.maximum(b_last + m_st, jnp.max(g_row, axis=1, keepdims=True))
        m_ref[h] = jnp.broadcast_to(m_new, (SUBLANES, LANES))
        return dict(
            dexp=jnp.exp(dmat - m_row),
            inter=jnp.exp(a - m_row),
            floor=jnp.exp(-m_row),
            w_row=jnp.exp(g_row - m_new),
            decay=jnp.exp(b_last + m_st - m_new),
        )

    def head_proj(h):
        q = _dot(hn(), wq_ref[:, h * DK:(h + 1) * DK])
        kt = lax.dot_general(wkt_ref[h * DK:(h + 1) * DK, :], hn(),
                             (((1,), (1,)), ((), ())), preferred_element_type=F32)
        v = _dot(hn(), wv_ref[:, h * DV:(h + 1) * DV]).astype(BF16)
        vx = jnp.concatenate([v, jnp.ones((TILE, LANES), BF16)], axis=1)
        return q, kt, vx

    def head_scores(q, kt):
        return _dot(q.astype(BF16), kt.astype(BF16))

    def head_num(h, q, vx, s, dk):
        pmat = (s * dk["dexp"]).astype(BF16)
        qs = (q * dk["inter"]).astype(BF16)
        numx = _dot(jnp.concatenate([pmat, qs], axis=1),
                    jnp.concatenate([vx, c_ref[h].astype(BF16)], axis=0))
        den = numx[:, DV:DV + 1]
        return numx[:, 0:DV] / jnp.maximum(jnp.abs(den), dk["floor"])

    def head_ktw(kt, dk):
        return (kt * dk["w_row"]).astype(BF16)

    def head_upd(h, ktw, vx, dk):
        c_ref[h] = dk["decay"] * c_ref[h] + _dot(ktw, vx)

    def head_gate(h):
        og = _dot(hn(), wog_ref[:, h * DV:(h + 1) * DV])
        zb = _dot(hn(), wzb_ref[:, h * DV:(h + 1) * DV])
        return _sigmoid(og) * _silu(zb)

    def head_out(h, hh, gate):
        yb_blk = (_rms(hh, ghead_ref[:, h * DV:(h + 1) * DV]) * gate).astype(BF16)
        contrib = _dot(yb_blk, wbout_ref[h * DV:(h + 1) * DV, :])
        if h == 0:
            yb_ref[...] = contrib
        else:
            yb_ref[...] += contrib

    hnp = hnp_ref[...]
    ga = _dot(hnp, wgab_ref[:, 0:D_MODEL])
    gb = _dot(hnp, wgab_ref[:, D_MODEL:2 * D_MODEL])
    pe = _dot(pp_ref[...].astype(BF16), wple_ref[...])
    hn_ref[...] = _rms(x_ref[...], gmix_ref[...]).astype(BF16)
    gparts = gate_proj()
    conv0 = conv_in(0)
    gcol, grow = gate_cumsum(*gparts)
    merged = (_sigmoid(ga) * ya_ref[...] + _sigmoid(gb) * yb_ref[...]).astype(BF16)
    ya0 = conv_gate(0, conv0)
    dk0 = head_decay(0, gcol, grow)
    x1 = xp_ref[...] + _dot(merged, wo_ref[...])
    r1 = _rms(x1, gple_ref[...]).astype(BF16)
    conv1 = conv_in(1)
    conv_out(0, ya0)
    dk1 = head_decay(1, gcol, grow)
    ple_gate = _sigmoid(_dot(r1, wpg_ref[...]))
    out_ref[...] = _rms(x1 + ple_gate * pe, gfinal_ref[...])
    ya1 = conv_gate(1, conv1)

    q0, kt0, vx0 = head_proj(0)
    conv_out(1, ya1)
    s0 = head_scores(q0, kt0)
    q1, kt1, vx1 = head_proj(1)
    hh0 = head_num(0, q0, vx0, s0, dk0)
    ktw0 = head_ktw(kt0, dk0)
    s1 = head_scores(q1, kt1)
    dk2 = head_decay(2, gcol, grow)
    gate0 = head_gate(0)
    q2, kt2, vx2 = head_proj(2)
    head_out(0, hh0, gate0)
    hh1 = head_num(1, q1, vx1, s1, dk1)
    ktw1 = head_ktw(kt1, dk1)
    s2 = head_scores(q2, kt2)
    dk3 = head_decay(3, gcol, grow)
    gate1 = head_gate(1)
    q3, kt3, vx3 = head_proj(3)
    head_out(1, hh1, gate1)
    hh2 = head_num(2, q2, vx2, s2, dk2)
    ktw2 = head_ktw(kt2, dk2)
    s3 = head_scores(q3, kt3)
    gate2 = head_gate(2)
    head_out(2, hh2, gate2)
    hh3 = head_num(3, q3, vx3, s3, dk3)
    ktw3 = head_ktw(kt3, dk3)
    gate3 = head_gate(3)
    head_upd(0, ktw0, vx0, dk0)
    head_upd(1, ktw1, vx1, dk1)
    head_upd(2, ktw2, vx2, dk2)
    head_out(3, hh3, gate3)
    head_upd(3, ktw3, vx3, dk3)
    hnp_ref[...] = hn()


def _const_spec(shape):
    return pl.BlockSpec(shape, lambda s: (0,) * len(shape), pipeline_mode=pl.Buffered(1))


def kernel(x, p, g_mix, w_in, conv_w, conv_b, w_a_out, b_gates, g_head, w_b_out, w_o, g_ple,
           w_ple_gate, w_ple, g_final):
    bsz, seq, _ = x.shape
    assert w_in.shape[0] == 1 and seq % TILE == 0
    tiles_per_row = seq // TILE
    n_tiles = bsz * tiles_per_row
    assert w_in.shape[2] == 4 * CONV_WIDTH + 2 * QK_DIM + 3 * V_DIM + 2 * N_HEADS + 2 * D_MODEL
    wt = jnp.transpose(w_in[0])
    b_g = jnp.pad(b_gates[0], (0, LANES - 2 * N_HEADS)).reshape(1, LANES)

    hbm_weights = (wt, w_a_out[0], w_b_out[0], w_o[0], w_ple_gate[0], w_ple[0])
    consts = (
        g_mix[0].reshape(1, D_MODEL), b_g, conv_w[0], conv_b[0].reshape(1, CONV_WIDTH),
        g_head[0].reshape(1, V_DIM), g_ple[0].reshape(1, D_MODEL), g_final.reshape(1, D_MODEL),
    )

    def cur_map(s):
        c = jnp.minimum(s, n_tiles - 1)
        return (c // tiles_per_row, c % tiles_per_row, 0)

    def prev_map(s):
        c = jnp.maximum(s - 1, 0)
        return (c // tiles_per_row, c % tiles_per_row, 0)

    in_specs = [
        pl.BlockSpec((None, TILE, D_MODEL), cur_map),
        pl.BlockSpec((None, TILE, D_MODEL), prev_map),
        pl.BlockSpec((None, TILE, PLE_DIM), prev_map),
    ] + [pl.BlockSpec(memory_space=pl.ANY)] * len(hbm_weights) + [_const_spec(a.shape) for a in consts]

    return pl.pallas_call(
        functools.partial(_block_kernel, tiles_per_row),
        grid=(n_tiles + 1,),
        in_specs=in_specs,
        out_specs=pl.BlockSpec((None, TILE, D_MODEL), prev_map),
        out_shape=jax.ShapeDtypeStruct(x.shape, x.dtype),
        scratch_shapes=[
            pltpu.VMEM((N_HEADS, DK, DVX), F32),
            pltpu.VMEM((N_HEADS, SUBLANES, LANES), F32),
            pltpu.VMEM((TILE + SUBLANES, CONV_WIDTH), F32),
            pltpu.VMEM((TILE, D_MODEL), BF16),
            pltpu.VMEM((TILE, D_MODEL), BF16),
            pltpu.VMEM((TILE, D_MODEL), F32),
            pltpu.VMEM((TILE, D_MODEL), F32),
            pltpu.VMEM((D_MODEL, 4 * CONV_WIDTH), BF16),
            pltpu.VMEM((D_MODEL, QK_DIM), BF16),
            pltpu.VMEM((QK_DIM, D_MODEL), BF16),
            pltpu.VMEM((D_MODEL, V_DIM), BF16),
            pltpu.VMEM((D_MODEL, V_DIM), BF16),
            pltpu.VMEM((D_MODEL, V_DIM), BF16),
            pltpu.VMEM((D_MODEL, 2 * D_MODEL), BF16),
            pltpu.VMEM((D_MODEL, LANES), BF16),
            pltpu.VMEM((CONV_WIDTH, D_MODEL), BF16),
            pltpu.VMEM((V_DIM, D_MODEL), BF16),
            pltpu.VMEM((D_MODEL, D_MODEL), BF16),
            pltpu.VMEM((D_MODEL, D_MODEL), BF16),
            pltpu.VMEM((PLE_DIM, D_MODEL), BF16),
            pltpu.VMEM((2, W_ROWS, W_KBLK), F32),
            pltpu.VMEM((2 * N_HEADS, D_MODEL), F32),
            pltpu.SemaphoreType.DMA((2, W_SPLIT)),
        ],
        compiler_params=pltpu.CompilerParams(
            dimension_semantics=("arbitrary",),
            vmem_limit_bytes=VMEM_LIMIT_BYTES,
        ),
        name="hybrid_block",
    )(x, x, p[0], *hbm_weights, *consts)
```

```python
import functools

import jax
import jax.numpy as jnp
from jax import lax
from jax.experimental import pallas as pl
from jax.experimental.pallas import tpu as pltpu

D_MODEL = 1024
PLE_DIM = 256
CONV_WIDTH = 1024
CONV_K = 3
N_HEADS = 4
QK_DIM = 1024
V_DIM = 2048
DK = QK_DIM // N_HEADS
DV = V_DIM // N_HEADS
EPS = 1e-6

LANES = 128
SUBLANES = 8
TILE = 256
DVX = DV + LANES
CONV_BLK = 512
W_KBLK = 256
VMEM_LIMIT_BYTES = 61 * 1024 * 1024

BF16 = jnp.bfloat16
F32 = jnp.float32


def _dot(a, b):
    return jnp.dot(a, b, preferred_element_type=F32)


def _sigmoid(v):
    return 0.5 * jnp.tanh(0.5 * v) + 0.5


def _silu(v):
    return v * _sigmoid(v)


def _rms(v, g):
    return v * lax.rsqrt(jnp.mean(v * v, axis=-1, keepdims=True) + EPS) * g


def _load_weights(wt_hbm, waout_hbm, wbout_hbm, wo_hbm, wpg_hbm, wple_hbm,
                  win_ref, wkt_ref, wgates_ref, waout_ref, wbout_ref, wo_ref, wpg_ref, wple_ref,
                  slots, gstage_ref, sem):
    depth = len(slots)
    eye = (lax.broadcasted_iota(jnp.int32, (W_KBLK, W_KBLK), 0)
           == lax.broadcasted_iota(jnp.int32, (W_KBLK, W_KBLK), 1)).astype(BF16)

    def copy(src_hbm, row, s):
        return pltpu.make_async_copy(src_hbm.at[pl.ds(row, W_KBLK), :], slots[s], sem.at[s])

    k_row0 = 4 * CONV_WIDTH + QK_DIM
    gate_row0 = k_row0 + QK_DIM + 3 * V_DIM
    n_pieces = win_ref.shape[1] // W_KBLK
    assert n_pieces % depth == 0

    def src_row(p):
        r = p * W_KBLK
        r = r + jnp.where(r >= k_row0, QK_DIM, 0)
        return pl.multiple_of(r + jnp.where(r >= gate_row0, 2 * N_HEADS, 0), SUBLANES)

    for s in range(depth):
        copy(wt_hbm, src_row(s), s).start(priority=s % 2)

    def group(g, carry):
        for s in range(depth):
            p = g * depth + s
            copy(wt_hbm, src_row(p), s).wait()
            blk = slots[s][...]
            is_q = jnp.logical_and(p * W_KBLK >= 4 * CONV_WIDTH, p * W_KBLK < k_row0)
            blk = blk * jnp.where(is_q, DK ** -0.5, 1.0).astype(F32)
            blk_t = lax.dot_general(blk.astype(BF16), eye, (((0,), (0,)), ((), ())),
                                    preferred_element_type=F32)
            win_ref[:, pl.ds(pl.multiple_of(p * W_KBLK, W_KBLK), W_KBLK)] = blk_t.astype(BF16)

            @pl.when(p + depth < n_pieces)
            def _():
                copy(wt_hbm, src_row(p + depth), s).start(priority=s % 2)
        return carry

    lax.fori_loop(0, n_pieces // depth, group, 0)

    plain = [(wt_hbm, k_row0 + r, wkt_ref, r) for r in range(0, QK_DIM, W_KBLK)]
    for src_hbm, dst_ref in ((waout_hbm, waout_ref), (wbout_hbm, wbout_ref), (wo_hbm, wo_ref),
                             (wpg_hbm, wpg_ref), (wple_hbm, wple_ref)):
        assert dst_ref.shape[1] == D_MODEL and dst_ref.shape[0] % W_KBLK == 0
        plain += [(src_hbm, r, dst_ref, r) for r in range(0, dst_ref.shape[0], W_KBLK)]
    for i in range(min(depth, len(plain))):
        copy(plain[i][0], plain[i][1], i % depth).start(priority=i % 2)
    for i, (src_hbm, row, dst_ref, dst_row) in enumerate(plain):
        copy(src_hbm, row, i % depth).wait()
        dst_ref[dst_row:dst_row + W_KBLK, :] = slots[i % depth][...].astype(BF16)
        if i + depth < len(plain):
            nxt = plain[i + depth]
            copy(nxt[0], nxt[1], i % depth).start(priority=i % 2)

    gcopy = pltpu.make_async_copy(wt_hbm.at[pl.ds(gate_row0, 2 * N_HEADS), :], gstage_ref, sem.at[0])
    gcopy.start()
    gcopy.wait()
    gpad = jnp.concatenate([gstage_ref[...], jnp.zeros((LANES - 2 * N_HEADS, D_MODEL), F32)], axis=0)
    wgates_ref[...] = gpad.T.astype(BF16)


def _block_kernel(tiles_per_row,
                  x_ref, xp_ref, pp_ref, wt_hbm, waout_hbm, wbout_hbm, wo_hbm, wpg_hbm, wple_hbm,
                  gmix_ref, bgates_ref, convw_ref, convb_ref, ghead_ref, gple_ref, gfinal_ref,
                  out_ref,
                  c_ref, m_ref, u_ref, hn_ref, hnp_ref, ya_ref, yb_ref,
                  win_ref, wkt_ref, wgates_ref, waout_ref, wbout_ref, wo_ref, wpg_ref, wple_ref,
                  stage_ref, gstage_ref, dma_sem):
    step = pl.program_id(0)
    cur = jnp.minimum(step, pl.num_programs(0) - 2)

    o = 0
    wconv_ref = win_ref.at[:, o:o + 4 * CONV_WIDTH]; o += 4 * CONV_WIDTH
    wq_ref = win_ref.at[:, o:o + QK_DIM]; o += QK_DIM
    wv_ref = win_ref.at[:, o:o + V_DIM]; o += V_DIM
    wog_ref = win_ref.at[:, o:o + V_DIM]; o += V_DIM
    wzb_ref = win_ref.at[:, o:o + V_DIM]; o += V_DIM
    wgab_ref = win_ref.at[:, o:o + 2 * D_MODEL]

    @pl.when(step == 0)
    def _():
        _load_weights(wt_hbm, waout_hbm, wbout_hbm, wo_hbm, wpg_hbm, wple_hbm,
                      win_ref, wkt_ref, wgates_ref, waout_ref, wbout_ref, wo_ref, wpg_ref, wple_ref,
                      [stage_ref.at[0], stage_ref.at[1], ya_ref, yb_ref], gstage_ref, dma_sem)
        for ref in (ya_ref, yb_ref, hnp_ref):
            ref[...] = jnp.zeros_like(ref)

    @pl.when(lax.rem(cur, tiles_per_row) == 0)
    def _():
        c_ref[...] = jnp.zeros_like(c_ref)
        m_ref[...] = jnp.full_like(m_ref, -jnp.inf)
        u_ref[0:SUBLANES, :] = jnp.zeros((SUBLANES, CONV_WIDTH), F32)

    def hn():
        return hn_ref[...]

    def conv_in(cb):
        c0 = cb * CONV_BLK
        cs = slice(c0, c0 + CONV_BLK)
        xa = _dot(hn(), wconv_ref[:, 0 * CONV_WIDTH + c0:0 * CONV_WIDTH + c0 + CONV_BLK])
        ca = _dot(hn(), wconv_ref[:, 2 * CONV_WIDTH + c0:2 * CONV_WIDTH + c0 + CONV_BLK])
        u_ref[SUBLANES:SUBLANES + TILE, cs] = ca * xa
        u0 = u_ref[SUBLANES:SUBLANES + TILE, cs]
        u1 = u_ref[SUBLANES - 1:SUBLANES - 1 + TILE, cs]
        u2 = u_ref[SUBLANES - 2:SUBLANES - 2 + TILE, cs]
        conv = (u2 * convw_ref[0:1, cs] + u1 * convw_ref[1:2, cs] + u0 * convw_ref[2:3, cs]
                + convb_ref[:, cs])
        u_ref[0:SUBLANES, cs] = u_ref[TILE:TILE + SUBLANES, cs]
        return conv

    def conv_gate(cb, conv):
        c0 = cb * CONV_BLK
        ba = _dot(hn(), wconv_ref[:, 1 * CONV_WIDTH + c0:1 * CONV_WIDTH + c0 + CONV_BLK])
        za = _dot(hn(), wconv_ref[:, 3 * CONV_WIDTH + c0:3 * CONV_WIDTH + c0 + CONV_BLK])
        return (ba * conv * _silu(za)).astype(BF16)

    def conv_out(cb, ya_blk):
        contrib = _dot(ya_blk, waout_ref[cb * CONV_BLK:(cb + 1) * CONV_BLK, :])
        if cb == 0:
            ya_ref[...] = contrib
        else:
            ya_ref[...] += contrib

    row_i = lax.broadcasted_iota(jnp.int32, (TILE, TILE), 0)
    col_i = lax.broadcasted_iota(jnp.int32, (TILE, TILE), 1)
    causal = row_i >= col_i

    def gate_proj():
        gates = _dot(hn(), wgates_ref[...]) + bgates_ref[...]
        lf = jnp.minimum(gates, 0.0) - jnp.log1p(jnp.exp(-jnp.abs(gates)))
        lf_hi = lf.astype(BF16)
        lf_lo = (lf - lf_hi.astype(F32)).astype(BF16)
        return gates, lf_hi, lf_lo

    def gate_cumsum(gates, lf_hi, lf_lo):
        tri = causal.astype(BF16)
        bcum = _dot(tri, lf_hi) + _dot(tri, lf_lo)
        lane = lax.broadcasted_iota(jnp.int32, (TILE, LANES), 1)
        gcol = jnp.where(lane < N_HEADS, gates, bcum)
        return gcol, gcol.T

    def head_decay(h, gcol, grow):
        li_row = grow[h:h + 1, :]
        b_row = grow[N_HEADS + h:N_HEADS + h + 1, :]
        b_col = gcol[:, N_HEADS + h:N_HEADS + h + 1]
        m_st = m_ref[h, 0:1, 0:1]
        dmat = jnp.where(causal, b_col - b_row + li_row, -jnp.inf)
        a = b_col + m_st
        m_row = jnp.maximum(a, jnp.max(dmat, axis=1, keepdims=True))
        b_last = b_row[:, TILE - 1:TILE]
        g_row = b_last - b_row + li_row
        m_new = jnp.maximum(b_last + m_st, jnp.max(g_row, axis=1, keepdims=True))
        m_ref[h] = jnp.broadcast_to(m_new, (SUBLANES, LANES))
        return dict(
            dexp=jnp.exp(dmat - m_row),
            inter=jnp.exp(a - m_row),
            floor=jnp.exp(-m_row),
            w_row=jnp.exp(g_row - m_new),
            decay=jnp.exp(b_last + m_st - m_new),
        )

    def head_proj(h):
        q = _dot(hn(), wq_ref[:, h * DK:(h + 1) * DK])
        kt = lax.dot_general(wkt_ref[h * DK:(h + 1) * DK, :], hn(),
                             (((1,), (1,)), ((), ())), preferred_element_type=F32)
        v = _dot(hn(), wv_ref[:, h * DV:(h + 1) * DV]).astype(BF16)
        vx = jnp.concatenate([v, jnp.ones((TILE, LANES), BF16)], axis=1)
        return q, kt, vx

    def head_scores(q, kt):
        return _dot(q.astype(BF16), kt.astype(BF16))

    def head_num(h, q, vx, s, dk):
        pmat = (s * dk["dexp"]).astype(BF16)
        qs = (q * dk["inter"]).astype(BF16)
        numx = _dot(jnp.concatenate([pmat, qs], axis=1),
                    jnp.concatenate([vx, c_ref[h].astype(BF16)], axis=0))
        den = numx[:, DV:DV + 1]
        return numx[:, 0:DV] / jnp.maximum(jnp.abs(den), dk["floor"])

    def head_ktw(kt, dk):
        return (kt * dk["w_row"]).astype(BF16)

    def head_upd(h, ktw, vx, dk):
        c_ref[h] = dk["decay"] * c_ref[h] + _dot(ktw, vx)

    def head_gate(h):
        og = _dot(hn(), wog_ref[:, h * DV:(h + 1) * DV])
        zb = _dot(hn(), wzb_ref[:, h * DV:(h + 1) * DV])
        return _sigmoid(og) * _silu(zb)

    def head_out(h, hh, gate):
        yb_blk = (_rms(hh, ghead_ref[:, h * DV:(h + 1) * DV]) * gate).astype(BF16)
        contrib = _dot(yb_blk, wbout_ref[h * DV:(h + 1) * DV, :])
        if h == 0:
            yb_ref[...] = contrib
        else:
            yb_ref[...] += contrib

    hnp = hnp_ref[...]
    ga = _dot(hnp, wgab_ref[:, 0:D_MODEL])
    gb = _dot(hnp, wgab_ref[:, D_MODEL:2 * D_MODEL])
    pe = _dot(pp_ref[...].astype(BF16), wple_ref[...])
    hn_ref[...] = _rms(x_ref[...], gmix_ref[...]).astype(BF16)
    gparts = gate_proj()
    conv0 = conv_in(0)
    gcol, grow = gate_cumsum(*gparts)
    merged = (_sigmoid(ga) * ya_ref[...] + _sigmoid(gb) * yb_ref[...]).astype(BF16)
    ya0 = conv_gate(0, conv0)
    dk0 = head_decay(0, gcol, grow)
    x1 = xp_ref[...] + _dot(merged, wo_ref[...])
    r1 = _rms(x1, gple_ref[...]).astype(BF16)
    conv1 = conv_in(1)
    conv_out(0, ya0)
    dk1 = head_decay(1, gcol, grow)
    ple_gate = _sigmoid(_dot(r1, wpg_ref[...]))
    out_ref[...] = _rms(x1 + ple_gate * pe, gfinal_ref[...])
    ya1 = conv_gate(1, conv1)

    q0, kt0, vx0 = head_proj(0)
    conv_out(1, ya1)
    s0 = head_scores(q0, kt0)
    q1, kt1, vx1 = head_proj(1)
    hh0 = head_num(0, q0, vx0, s0, dk0)
    ktw0 = head_ktw(kt0, dk0)
    s1 = head_scores(q1, kt1)
    dk2 = head_decay(2, gcol, grow)
    gate0 = head_gate(0)
    q2, kt2, vx2 = head_proj(2)
    head_out(0, hh0, gate0)
    hh1 = head_num(1, q1, vx1, s1, dk1)
    ktw1 = head_ktw(kt1, dk1)
    s2 = head_scores(q2, kt2)
    dk3 = head_decay(3, gcol, grow)
    gate1 = head_gate(1)
    q3, kt3, vx3 = head_proj(3)
    head_out(1, hh1, gate1)
    hh2 = head_num(2, q2, vx2, s2, dk2)
    ktw2 = head_ktw(kt2, dk2)
    s3 = head_scores(q3, kt3)
    gate2 = head_gate(2)
    head_out(2, hh2, gate2)
    hh3 = head_num(3, q3, vx3, s3, dk3)
    ktw3 = head_ktw(kt3, dk3)
    gate3 = head_gate(3)
    head_upd(0, ktw0, vx0, dk0)
    head_upd(1, ktw1, vx1, dk1)
    head_upd(2, ktw2, vx2, dk2)
    head_out(3, hh3, gate3)
    head_upd(3, ktw3, vx3, dk3)
    hnp_ref[...] = hn()


def _const_spec(shape):
    return pl.BlockSpec(shape, lambda s: (0,) * len(shape), pipeline_mode=pl.Buffered(1))


def kernel(x, p, g_mix, w_in, conv_w, conv_b, w_a_out, b_gates, g_head, w_b_out, w_o, g_ple,
           w_ple_gate, w_ple, g_final):
    bsz, seq, _ = x.shape
    assert w_in.shape[0] == 1 and seq % TILE == 0
    tiles_per_row = seq // TILE
    n_tiles = bsz * tiles_per_row
    assert w_in.shape[2] == 4 * CONV_WIDTH + 2 * QK_DIM + 3 * V_DIM + 2 * N_HEADS + 2 * D_MODEL
    wt = jnp.transpose(w_in[0])
    b_g = jnp.pad(b_gates[0], (0, LANES - 2 * N_HEADS)).reshape(1, LANES)

    hbm_weights = (wt, w_a_out[0], w_b_out[0], w_o[0], w_ple_gate[0], w_ple[0])
    consts = (
        g_mix[0].reshape(1, D_MODEL), b_g, conv_w[0], conv_b[0].reshape(1, CONV_WIDTH),
        g_head[0].reshape(1, V_DIM), g_ple[0].reshape(1, D_MODEL), g_final.reshape(1, D_MODEL),
    )

    def cur_map(s):
        c = jnp.minimum(s, n_tiles - 1)
        return (c // tiles_per_row, c % tiles_per_row, 0)

    def prev_map(s):
        c = jnp.maximum(s - 1, 0)
        return (c // tiles_per_row, c % tiles_per_row, 0)

    in_specs = [
        pl.BlockSpec((None, TILE, D_MODEL), cur_map),
        pl.BlockSpec((None, TILE, D_MODEL), prev_map),
        pl.BlockSpec((None, TILE, PLE_DIM), prev_map),
    ] + [pl.BlockSpec(memory_space=pl.ANY)] * len(hbm_weights) + [_const_spec(a.shape) for a in consts]

    return pl.pallas_call(
        functools.partial(_block_kernel, tiles_per_row),
        grid=(n_tiles + 1,),
        in_specs=in_specs,
        out_specs=pl.BlockSpec((None, TILE, D_MODEL), prev_map),
        out_shape=jax.ShapeDtypeStruct(x.shape, x.dtype),
        scratch_shapes=[
            pltpu.VMEM((N_HEADS, DK, DVX), F32),
            pltpu.VMEM((N_HEADS, SUBLANES, LANES), F32),
            pltpu.VMEM((TILE + SUBLANES, CONV_WIDTH), F32),
            pltpu.VMEM((TILE, D_MODEL), BF16),
            pltpu.VMEM((TILE, D_MODEL), BF16),
            pltpu.VMEM((TILE, D_MODEL), F32),
            pltpu.VMEM((TILE, D_MODEL), F32),
            pltpu.VMEM((D_MODEL, 4 * CONV_WIDTH + QK_DIM + 3 * V_DIM + 2 * D_MODEL), BF16),
            pltpu.VMEM((QK_DIM, D_MODEL), BF16),
            pltpu.VMEM((D_MODEL, LANES), BF16),
            pltpu.VMEM((CONV_WIDTH, D_MODEL), BF16),
            pltpu.VMEM((V_DIM, D_MODEL), BF16),
            pltpu.VMEM((D_MODEL, D_MODEL), BF16),
            pltpu.VMEM((D_MODEL, D_MODEL), BF16),
            pltpu.VMEM((PLE_DIM, D_MODEL), BF16),
            pltpu.VMEM((2, W_KBLK, D_MODEL), F32),
            pltpu.VMEM((2 * N_HEADS, D_MODEL), F32),
            pltpu.SemaphoreType.DMA((4,)),
        ],
        compiler_params=pltpu.CompilerParams(
            dimension_semantics=("arbitrary",),
            vmem_limit_bytes=VMEM_LIMIT_BYTES,
        ),
        name="hybrid_block",
    )(x, x, p[0], *hbm_weights, *consts)
```

```python
import functools

import jax
import jax.numpy as jnp
from jax import lax
from jax.experimental import pallas as pl
from jax.experimental.pallas import tpu as pltpu

D_MODEL = 1024
PLE_DIM = 256
CONV_WIDTH = 1024
CONV_K = 3
N_HEADS = 4
QK_DIM = 1024
V_DIM = 2048
DK = QK_DIM // N_HEADS
DV = V_DIM // N_HEADS
EPS = 1e-6

LANES = 128
SUBLANES = 8
TILE = 256
DVX = DV + LANES
CONV_BLK = 512
W_KBLK = 256
VMEM_LIMIT_BYTES = 61 * 1024 * 1024

BF16 = jnp.bfloat16
F32 = jnp.float32


def _dot(a, b):
    return jnp.dot(a, b, preferred_element_type=F32)


def _sigmoid(v):
    return 0.5 * jnp.tanh(0.5 * v) + 0.5


def _silu(v):
    return v * _sigmoid(v)


def _rms(v, g):
    return v * lax.rsqrt(jnp.mean(v * v, axis=-1, keepdims=True) + EPS) * g


def _load_weights(wt_hbm, waout_hbm, wbout_hbm, wo_hbm, wpg_hbm, wple_hbm,
                  win_ref, wkt_ref, wgates_ref, waout_ref, wbout_ref, wo_ref, wpg_ref, wple_ref,
                  slots, gstage_ref, sem):
    depth = len(slots)
    eye = (lax.broadcasted_iota(jnp.int32, (W_KBLK, W_KBLK), 0)
           == lax.broadcasted_iota(jnp.int32, (W_KBLK, W_KBLK), 1)).astype(BF16)

    def copy(src_hbm, row, s):
        return pltpu.make_async_copy(src_hbm.at[pl.ds(row, W_KBLK), :], slots[s], sem.at[s])

    k_row0 = 4 * CONV_WIDTH + QK_DIM
    gate_row0 = k_row0 + QK_DIM + 3 * V_DIM
    n_pieces = win_ref.shape[1] // W_KBLK
    assert n_pieces % depth == 0

    def src_row(p):
        r = p * W_KBLK
        r = r + jnp.where(r >= k_row0, QK_DIM, 0)
        return pl.multiple_of(r + jnp.where(r >= gate_row0, 2 * N_HEADS, 0), SUBLANES)

    for s in range(depth):
        copy(wt_hbm, src_row(s), s).start(priority=s % 2)

    def group(g, carry):
        for s in range(depth):
            p = g * depth + s
            copy(wt_hbm, src_row(p), s).wait()
            blk = slots[s][...]
            is_q = jnp.logical_and(p * W_KBLK >= 4 * CONV_WIDTH, p * W_KBLK < k_row0)
            blk = blk * jnp.where(is_q, DK ** -0.5, 1.0).astype(F32)
            blk_t = lax.dot_general(blk.astype(BF16), eye, (((0,), (0,)), ((), ())),
                                    preferred_element_type=F32)
            win_ref[:, pl.ds(pl.multiple_of(p * W_KBLK, W_KBLK), W_KBLK)] = blk_t.astype(BF16)

            @pl.when(p + depth < n_pieces)
            def _():
                copy(wt_hbm, src_row(p + depth), s).start(priority=s % 2)
        return carry

    lax.fori_loop(0, n_pieces // depth, group, 0)

    plain = [(wt_hbm, k_row0 + r, wkt_ref, r) for r in range(0, QK_DIM, W_KBLK)]
    for src_hbm, dst_ref in ((waout_hbm, waout_ref), (wbout_hbm, wbout_ref), (wo_hbm, wo_ref),
                             (wpg_hbm, wpg_ref), (wple_hbm, wple_ref)):
        assert dst_ref.shape[1] == D_MODEL and dst_ref.shape[0] % W_KBLK == 0
        plain += [(src_hbm, r, dst_ref, r) for r in range(0, dst_ref.shape[0], W_KBLK)]
    for i in range(min(depth, len(plain))):
        copy(plain[i][0], plain[i][1], i % depth).start(priority=i % 2)
    for i, (src_hbm, row, dst_ref, dst_row) in enumerate(plain):
        copy(src_hbm, row, i % depth).wait()
        dst_ref[dst_row:dst_row + W_KBLK, :] = slots[i % depth][...].astype(BF16)
        if i + depth < len(plain):
            nxt = plain[i + depth]
            copy(nxt[0], nxt[1], i % depth).start(priority=i % 2)

    gcopy = pltpu.make_async_copy(wt_hbm.at[pl.ds(gate_row0, 2 * N_HEADS), :], gstage_ref, sem.at[0])
    gcopy.start()
    gcopy.wait()
    gpad = jnp.concatenate([gstage_ref[...], jnp.zeros((LANES - 2 * N_HEADS, D_MODEL), F32)], axis=0)
    wgates_ref[...] = gpad.T.astype(BF16)


def _block_kernel(tiles_per_row,
                  x_ref, xp_ref, pp_ref, wt_hbm, waout_hbm, wbout_hbm, wo_hbm, wpg_hbm, wple_hbm,
                  gmix_ref, bgates_ref, convw_ref, convb_ref, ghead_ref, gple_ref, gfinal_ref,
                  out_ref,
                  c_ref, m_ref, u_ref, hn_ref, hnp_ref, ya_ref, yb_ref,
                  win_ref, wkt_ref, wgates_ref, waout_ref, wbout_ref, wo_ref, wpg_ref, wple_ref,
                  stage_ref, gstage_ref, dma_sem):
    step = pl.program_id(0)

    o = 0
    wconv_ref = win_ref.at[:, o:o + 4 * CONV_WIDTH]; o += 4 * CONV_WIDTH
    wq_ref = win_ref.at[:, o:o + QK_DIM]; o += QK_DIM
    wv_ref = win_ref.at[:, o:o + V_DIM]; o += V_DIM
    wog_ref = win_ref.at[:, o:o + V_DIM]; o += V_DIM
    wzb_ref = win_ref.at[:, o:o + V_DIM]; o += V_DIM
    wgab_ref = win_ref.at[:, o:o + 2 * D_MODEL]

    @pl.when(step == 0)
    def _():
        _load_weights(wt_hbm, waout_hbm, wbout_hbm, wo_hbm, wpg_hbm, wple_hbm,
                      win_ref, wkt_ref, wgates_ref, waout_ref, wbout_ref, wo_ref, wpg_ref, wple_ref,
                      [stage_ref.at[0], stage_ref.at[1], ya_ref, yb_ref], gstage_ref, dma_sem)
        for ref in (ya_ref, yb_ref, hnp_ref):
            ref[...] = jnp.zeros_like(ref)

    @pl.when(lax.rem(step, tiles_per_row) == 0)
    def _():
        c_ref[...] = jnp.zeros_like(c_ref)
        m_ref[...] = jnp.full_like(m_ref, -jnp.inf)
        u_ref[0:SUBLANES, :] = jnp.zeros((SUBLANES, CONV_WIDTH), F32)

    def hn():
        return hn_ref[...]

    def conv_in(cb):
        c0 = cb * CONV_BLK
        cs = slice(c0, c0 + CONV_BLK)
        xa = _dot(hn(), wconv_ref[:, 0 * CONV_WIDTH + c0:0 * CONV_WIDTH + c0 + CONV_BLK])
        ca = _dot(hn(), wconv_ref[:, 2 * CONV_WIDTH + c0:2 * CONV_WIDTH + c0 + CONV_BLK])
        u_ref[SUBLANES:SUBLANES + TILE, cs] = ca * xa
        u0 = u_ref[SUBLANES:SUBLANES + TILE, cs]
        u1 = u_ref[SUBLANES - 1:SUBLANES - 1 + TILE, cs]
        u2 = u_ref[SUBLANES - 2:SUBLANES - 2 + TILE, cs]
        conv = (u2 * convw_ref[0:1, cs] + u1 * convw_ref[1:2, cs] + u0 * convw_ref[2:3, cs]
                + convb_ref[:, cs])
        u_ref[0:SUBLANES, cs] = u_ref[TILE:TILE + SUBLANES, cs]
        return conv

    def conv_gate(cb, conv):
        c0 = cb * CONV_BLK
        ba = _dot(hn(), wconv_ref[:, 1 * CONV_WIDTH + c0:1 * CONV_WIDTH + c0 + CONV_BLK])
        za = _dot(hn(), wconv_ref[:, 3 * CONV_WIDTH + c0:3 * CONV_WIDTH + c0 + CONV_BLK])
        return (ba * conv * _silu(za)).astype(BF16)

    def conv_out(cb, ya_blk):
        contrib = _dot(ya_blk, waout_ref[cb * CONV_BLK:(cb + 1) * CONV_BLK, :])
        if cb == 0:
            ya_ref[...] = contrib
        else:
            ya_ref[...] += contrib

    def causal_mask():
        return (lax.broadcasted_iota(jnp.int32, (TILE, TILE), 0)
                >= lax.broadcasted_iota(jnp.int32, (TILE, TILE), 1))

    def gate_proj():
        gates = _dot(hn(), wgates_ref[...]) + bgates_ref[...]
        lf = jnp.minimum(gates, 0.0) - jnp.log1p(jnp.exp(-jnp.abs(gates)))
        lf_hi = lf.astype(BF16)
        lf_lo = (lf - lf_hi.astype(F32)).astype(BF16)
        return gates, lf_hi, lf_lo

    def gate_cumsum(gates, lf_hi, lf_lo):
        tri = causal_mask().astype(BF16)
        bcum = _dot(tri, lf_hi) + _dot(tri, lf_lo)
        lane = lax.broadcasted_iota(jnp.int32, (TILE, LANES), 1)
        gcol = jnp.where(lane < N_HEADS, gates, bcum)
        return gcol, gcol.T

    def head_decay(h, gcol, grow):
        li_row = grow[h:h + 1, :]
        b_row = grow[N_HEADS + h:N_HEADS + h + 1, :]
        b_col = gcol[:, N_HEADS + h:N_HEADS + h + 1]
        m_st = m_ref[h, 0:1, 0:1]
        dmat = jnp.where(causal_mask(), b_col - b_row + li_row, -jnp.inf)
        a = b_col + m_st
        m_row = jnp.maximum(a, jnp.max(dmat, axis=1, keepdims=True))
        b_last = b_row[:, TILE - 1:TILE]
        g_row = b_last - b_row + li_row
        m_new = jnp.maximum(b_last + m_st, jnp.max(g_row, axis=1, keepdims=True))
        m_ref[h] = jnp.broadcast_to(m_new, (SUBLANES, LANES))
        return dict(
            dexp=jnp.exp(dmat - m_row),
            inter=jnp.exp(a - m_row),
            floor=jnp.exp(-m_row),
            w_row=jnp.exp(g_row - m_new),
            decay=jnp.exp(b_last + m_st - m_new),
        )

    def head_proj(h):
        q = _dot(hn(), wq_ref[:, h * DK:(h + 1) * DK])
        kt = lax.dot_general(wkt_ref[h * DK:(h + 1) * DK, :], hn(),
                             (((1,), (1,)), ((), ())), preferred_element_type=F32)
        v = _dot(hn(), wv_ref[:, h * DV:(h + 1) * DV]).astype(BF16)
        vx = jnp.concatenate([v, jnp.ones((TILE, LANES), BF16)], axis=1)
        return q, kt, vx

    def head_scores(q, kt):
        return _dot(q.astype(BF16), kt.astype(BF16))

    def head_num(h, q, vx, s, dk):
        pmat = (s * dk["dexp"]).astype(BF16)
        qs = (q * dk["inter"]).astype(BF16)
        numx = _dot(jnp.concatenate([pmat, qs], axis=1),
                    jnp.concatenate([vx, c_ref[h].astype(BF16)], axis=0))
        den = numx[:, DV:DV + 1]
        return numx[:, 0:DV] / jnp.maximum(jnp.abs(den), dk["floor"])

    def head_ktw(kt, dk):
        return (kt * dk["w_row"]).astype(BF16)

    def head_upd(h, ktw, vx, dk):
        c_ref[h] = dk["decay"] * c_ref[h] + _dot(ktw, vx)

    def head_gate(h):
        og = _dot(hn(), wog_ref[:, h * DV:(h + 1) * DV])
        zb = _dot(hn(), wzb_ref[:, h * DV:(h + 1) * DV])
        return _sigmoid(og) * _silu(zb)

    def head_out(h, hh, gate):
        yb_blk = (_rms(hh, ghead_ref[:, h * DV:(h + 1) * DV]) * gate).astype(BF16)
        contrib = _dot(yb_blk, wbout_ref[h * DV:(h + 1) * DV, :])
        if h == 0:
            yb_ref[...] = contrib
        else:
            yb_ref[...] += contrib

    def tail_gates():
        hnp = hnp_ref[...]
        return _dot(hnp, wgab_ref[:, 0:D_MODEL]), _dot(hnp, wgab_ref[:, D_MODEL:2 * D_MODEL])

    def tail_merge(ga, gb):
        return (_sigmoid(ga) * ya_ref[...] + _sigmoid(gb) * yb_ref[...]).astype(BF16)

    def tail_out_proj(merged):
        x1 = xp_ref[...] + _dot(merged, wo_ref[...])
        return x1, _rms(x1, gple_ref[...]).astype(BF16)

    def tail_finish(x1, r1):
        pe = _dot(pp_ref[...].astype(BF16), wple_ref[...])
        ple_gate = _sigmoid(_dot(r1, wpg_ref[...]))
        out_ref[...] = _rms(x1 + ple_gate * pe, gfinal_ref[...])

    @pl.when(step == pl.num_programs(0) - 1)
    def _():
        tail_finish(*tail_out_proj(tail_merge(*tail_gates())))

    @pl.when(step < pl.num_programs(0) - 1)
    def _():
        ga, gb = tail_gates()
        hn_ref[...] = _rms(x_ref[...], gmix_ref[...]).astype(BF16)
        gparts = gate_proj()
        conv0 = conv_in(0)
        x1, r1 = tail_out_proj(tail_merge(ga, gb))
        gcol, grow = gate_cumsum(*gparts)
        ya0 = conv_gate(0, conv0)
        dk0 = head_decay(0, gcol, grow)
        conv1 = conv_in(1)
        conv_out(0, ya0)
        dk1 = head_decay(1, gcol, grow)
        tail_finish(x1, r1)
        ya1 = conv_gate(1, conv1)

        q0, kt0, vx0 = head_proj(0)
        conv_out(1, ya1)
        s0 = head_scores(q0, kt0)
        q1, kt1, vx1 = head_proj(1)
        hh0 = head_num(0, q0, vx0, s0, dk0)
        ktw0 = head_ktw(kt0, dk0)
        s1 = head_scores(q1, kt1)
        dk2 = head_decay(2, gcol, grow)
        gate0 = head_gate(0)
        q2, kt2, vx2 = head_proj(2)
        head_out(0, hh0, gate0)
        hh1 = head_num(1, q1, vx1, s1, dk1)
        ktw1 = head_ktw(kt1, dk1)
        s2 = head_scores(q2, kt2)
        dk3 = head_decay(3, gcol, grow)
        gate1 = head_gate(1)
        q3, kt3, vx3 = head_proj(3)
        head_out(1, hh1, gate1)
        hh2 = head_num(2, q2, vx2, s2, dk2)
        ktw2 = head_ktw(kt2, dk2)
        s3 = head_scores(q3, kt3)
        gate2 = head_gate(2)
        head_out(2, hh2, gate2)
        hh3 = head_num(3, q3, vx3, s3, dk3)
        ktw3 = head_ktw(kt3, dk3)
        gate3 = head_gate(3)
        head_upd(0, ktw0, vx0, dk0)
        head_upd(1, ktw1, vx1, dk1)
        head_upd(2, ktw2, vx2, dk2)
        head_out(3, hh3, gate3)
        head_upd(3, ktw3, vx3, dk3)
        hnp_ref[...] = hn()


def _const_spec(shape):
    return pl.BlockSpec(shape, lambda s: (0,) * len(shape), pipeline_mode=pl.Buffered(1))


def kernel(x, p, g_mix, w_in, conv_w, conv_b, w_a_out, b_gates, g_head, w_b_out, w_o, g_ple,
           w_ple_gate, w_ple, g_final):
    bsz, seq, _ = x.shape
    assert w_in.shape[0] == 1 and seq % TILE == 0
    tiles_per_row = seq // TILE
    n_tiles = bsz * tiles_per_row
    assert w_in.shape[2] == 4 * CONV_WIDTH + 2 * QK_DIM + 3 * V_DIM + 2 * N_HEADS + 2 * D_MODEL
    wt = jnp.transpose(w_in[0])
    b_g = jnp.pad(b_gates[0], (0, LANES - 2 * N_HEADS)).reshape(1, LANES)

    hbm_weights = (wt, w_a_out[0], w_b_out[0], w_o[0], w_ple_gate[0], w_ple[0])
    consts = (
        g_mix[0].reshape(1, D_MODEL), b_g, conv_w[0], conv_b[0].reshape(1, CONV_WIDTH),
        g_head[0].reshape(1, V_DIM), g_ple[0].reshape(1, D_MODEL), g_final.reshape(1, D_MODEL),
    )

    def cur_map(s):
        c = jnp.minimum(s, n_tiles - 1)
        return (c // tiles_per_row, c % tiles_per_row, 0)

    def prev_map(s):
        c = jnp.maximum(s - 1, 0)
        return (c // tiles_per_row, c % tiles_per_row, 0)

    in_specs = [
        pl.BlockSpec((None, TILE, D_MODEL), cur_map),
        pl.BlockSpec((None, TILE, D_MODEL), prev_map),
        pl.BlockSpec((None, TILE, PLE_DIM), prev_map),
    ] + [pl.BlockSpec(memory_space=pl.ANY)] * len(hbm_weights) + [_const_spec(a.shape) for a in consts]

    return pl.pallas_call(
        functools.partial(_block_kernel, tiles_per_row),
        grid=(n_tiles + 1,),
        in_specs=in_specs,
        out_specs=pl.BlockSpec((None, TILE, D_MODEL), prev_map),
        out_shape=jax.ShapeDtypeStruct(x.shape, x.dtype),
        scratch_shapes=[
            pltpu.VMEM((N_HEADS, DK, DVX), F32),
            pltpu.VMEM((N_HEADS, SUBLANES, LANES), F32),
            pltpu.VMEM((TILE + SUBLANES, CONV_WIDTH), F32),
            pltpu.VMEM((TILE, D_MODEL), BF16),
            pltpu.VMEM((TILE, D_MODEL), BF16),
            pltpu.VMEM((TILE, D_MODEL), F32),
            pltpu.VMEM((TILE, D_MODEL), F32),
            pltpu.VMEM((D_MODEL, 4 * CONV_WIDTH + QK_DIM + 3 * V_DIM + 2 * D_MODEL), BF16),
            pltpu.VMEM((QK_DIM, D_MODEL), BF16),
            pltpu.VMEM((D_MODEL, LANES), BF16),
            pltpu.VMEM((CONV_WIDTH, D_MODEL), BF16),
            pltpu.VMEM((V_DIM, D_MODEL), BF16),
            pltpu.VMEM((D_MODEL, D_MODEL), BF16),
            pltpu.VMEM((D_MODEL, D_MODEL), BF16),
            pltpu.VMEM((PLE_DIM, D_MODEL), BF16),
            pltpu.VMEM((2, W_KBLK, D_MODEL), F32),
            pltpu.VMEM((2 * N_HEADS, D_MODEL), F32),
            pltpu.SemaphoreType.DMA((4,)),
        ],
        compiler_params=pltpu.CompilerParams(
            dimension_semantics=("arbitrary",),
            vmem_limit_bytes=VMEM_LIMIT_BYTES,
        ),
        name="hybrid_block",
    )(x, x, p[0], *hbm_weights, *consts)
```

```python
import functools

import jax
import jax.numpy as jnp
from jax import lax
from jax.experimental import pallas as pl
from jax.experimental.pallas import tpu as pltpu

D_MODEL = 1024
PLE_DIM = 256
CONV_WIDTH = 1024
CONV_K = 3
N_HEADS = 4
QK_DIM = 1024
V_DIM = 2048
DK = QK_DIM // N_HEADS
DV = V_DIM // N_HEADS
EPS = 1e-6

LANES = 128
SUBLANES = 8
TILE = 256
CONV_BLK = 512
W_KBLK = 256
VMEM_LIMIT_BYTES = 61 * 1024 * 1024

BF16 = jnp.bfloat16
F32 = jnp.float32


def _dot(a, b):
    return jnp.dot(a, b, preferred_element_type=F32)


def _sigmoid(v):
    return 0.5 * jnp.tanh(0.5 * v) + 0.5


def _silu(v):
    return v * _sigmoid(v)


def _rms(v, g):
    return v * lax.rsqrt(jnp.mean(v * v, axis=-1, keepdims=True) + EPS) * g


def _load_weights(wt_hbm, waout_hbm, wbout_hbm, wo_hbm, wpg_hbm, wple_hbm,
                  win_ref, wkt_ref, wgates_ref, waout_ref, wbout_ref, wo_ref, wpg_ref, wple_ref,
                  slots, gstage_ref, sem):
    depth = len(slots)
    eye = (lax.broadcasted_iota(jnp.int32, (W_KBLK, W_KBLK), 0)
           == lax.broadcasted_iota(jnp.int32, (W_KBLK, W_KBLK), 1)).astype(BF16)

    def copy(src_hbm, row, s):
        return pltpu.make_async_copy(src_hbm.at[pl.ds(row, W_KBLK), :], slots[s], sem.at[s])

    k_row0 = 4 * CONV_WIDTH + QK_DIM
    gate_row0 = k_row0 + QK_DIM + 3 * V_DIM
    n_pieces = win_ref.shape[1] // W_KBLK
    assert n_pieces % depth == 0

    def src_row(p):
        r = p * W_KBLK
        r = r + jnp.where(r >= k_row0, QK_DIM, 0)
        return pl.multiple_of(r + jnp.where(r >= gate_row0, 2 * N_HEADS, 0), SUBLANES)

    for s in range(depth):
        copy(wt_hbm, src_row(s), s).start(priority=s % 2)

    def group(g, carry):
        for s in range(depth):
            p = g * depth + s
            copy(wt_hbm, src_row(p), s).wait()
            blk = slots[s][...]
            is_q = jnp.logical_and(p * W_KBLK >= 4 * CONV_WIDTH, p * W_KBLK < k_row0)
            blk = blk * jnp.where(is_q, DK ** -0.5, 1.0).astype(F32)
            blk_t = lax.dot_general(blk.astype(BF16), eye, (((0,), (0,)), ((), ())),
                                    preferred_element_type=F32)
            win_ref[:, pl.ds(pl.multiple_of(p * W_KBLK, W_KBLK), W_KBLK)] = blk_t.astype(BF16)

            @pl.when(p + depth < n_pieces)
            def _():
                copy(wt_hbm, src_row(p + depth), s).start(priority=s % 2)
        return carry

    lax.fori_loop(0, n_pieces // depth, group, 0)

    plain = [(wt_hbm, k_row0 + r, wkt_ref, r) for r in range(0, QK_DIM, W_KBLK)]
    for src_hbm, dst_ref in ((waout_hbm, waout_ref), (wbout_hbm, wbout_ref), (wo_hbm, wo_ref),
                             (wpg_hbm, wpg_ref), (wple_hbm, wple_ref)):
        assert dst_ref.shape[1] == D_MODEL and dst_ref.shape[0] % W_KBLK == 0
        plain += [(src_hbm, r, dst_ref, r) for r in range(0, dst_ref.shape[0], W_KBLK)]
    for i in range(min(depth, len(plain))):
        copy(plain[i][0], plain[i][1], i % depth).start(priority=i % 2)
    for i, (src_hbm, row, dst_ref, dst_row) in enumerate(plain):
        copy(src_hbm, row, i % depth).wait()
        dst_ref[dst_row:dst_row + W_KBLK, :] = slots[i % depth][...].astype(BF16)
        if i + depth < len(plain):
            nxt = plain[i + depth]
            copy(nxt[0], nxt[1], i % depth).start(priority=i % 2)

    gcopy = pltpu.make_async_copy(wt_hbm.at[pl.ds(gate_row0, 2 * N_HEADS), :], gstage_ref, sem.at[0])
    gcopy.start()
    gcopy.wait()
    gpad = jnp.concatenate([gstage_ref[...], jnp.zeros((LANES - 2 * N_HEADS, D_MODEL), F32)], axis=0)
    wgates_ref[...] = gpad.T.astype(BF16)


def _block_kernel(tiles_per_row,
                  x_ref, xp_ref, pp_ref, wt_hbm, waout_hbm, wbout_hbm, wo_hbm, wpg_hbm, wple_hbm,
                  gmix_ref, bgates_ref, convw_ref, convb_ref, ghead_ref, gple_ref, gfinal_ref,
                  out_ref,
                  c_ref, n_ref, m_ref, u_ref, hn_ref, hnp_ref, ya_ref, yb_ref,
                  win_ref, wkt_ref, wgates_ref, waout_ref, wbout_ref, wo_ref, wpg_ref, wple_ref,
                  stage_ref, gstage_ref, dma_sem):
    step = pl.program_id(0)

    o = 0
    wconv_ref = win_ref.at[:, o:o + 4 * CONV_WIDTH]; o += 4 * CONV_WIDTH
    wq_ref = win_ref.at[:, o:o + QK_DIM]; o += QK_DIM
    wv_ref = win_ref.at[:, o:o + V_DIM]; o += V_DIM
    wog_ref = win_ref.at[:, o:o + V_DIM]; o += V_DIM
    wzb_ref = win_ref.at[:, o:o + V_DIM]; o += V_DIM
    wgab_ref = win_ref.at[:, o:o + 2 * D_MODEL]

    @pl.when(step == 0)
    def _():
        _load_weights(wt_hbm, waout_hbm, wbout_hbm, wo_hbm, wpg_hbm, wple_hbm,
                      win_ref, wkt_ref, wgates_ref, waout_ref, wbout_ref, wo_ref, wpg_ref, wple_ref,
                      [stage_ref.at[0], stage_ref.at[1], ya_ref, yb_ref], gstage_ref, dma_sem)
        for ref in (ya_ref, yb_ref, hnp_ref):
            ref[...] = jnp.zeros_like(ref)

    @pl.when(lax.rem(step, tiles_per_row) == 0)
    def _():
        c_ref[...] = jnp.zeros_like(c_ref)
        n_ref[...] = jnp.zeros_like(n_ref)
        m_ref[...] = jnp.full_like(m_ref, -jnp.inf)
        u_ref[0:SUBLANES, :] = jnp.zeros((SUBLANES, CONV_WIDTH), F32)

    def hn():
        return hn_ref[...]

    def conv_in(cb):
        c0 = cb * CONV_BLK
        cs = slice(c0, c0 + CONV_BLK)
        xa = _dot(hn(), wconv_ref[:, 0 * CONV_WIDTH + c0:0 * CONV_WIDTH + c0 + CONV_BLK])
        ca = _dot(hn(), wconv_ref[:, 2 * CONV_WIDTH + c0:2 * CONV_WIDTH + c0 + CONV_BLK])
        u_ref[SUBLANES:SUBLANES + TILE, cs] = ca * xa
        u0 = u_ref[SUBLANES:SUBLANES + TILE, cs]
        u1 = u_ref[SUBLANES - 1:SUBLANES - 1 + TILE, cs]
        u2 = u_ref[SUBLANES - 2:SUBLANES - 2 + TILE, cs]
        conv = (u2 * convw_ref[0:1, cs] + u1 * convw_ref[1:2, cs] + u0 * convw_ref[2:3, cs]
                + convb_ref[:, cs])
        u_ref[0:SUBLANES, cs] = u_ref[TILE:TILE + SUBLANES, cs]
        return conv

    def conv_gate(cb, conv):
        c0 = cb * CONV_BLK
        ba = _dot(hn(), wconv_ref[:, 1 * CONV_WIDTH + c0:1 * CONV_WIDTH + c0 + CONV_BLK])
        za = _dot(hn(), wconv_ref[:, 3 * CONV_WIDTH + c0:3 * CONV_WIDTH + c0 + CONV_BLK])
        return (ba * conv * _silu(za)).astype(BF16)

    def conv_out(cb, ya_blk):
        contrib = _dot(ya_blk, waout_ref[cb * CONV_BLK:(cb + 1) * CONV_BLK, :])
        if cb == 0:
            ya_ref[...] = contrib
        else:
            ya_ref[...] += contrib

    def causal_mask():
        return (lax.broadcasted_iota(jnp.int32, (TILE, TILE), 0)
                >= lax.broadcasted_iota(jnp.int32, (TILE, TILE), 1))

    def gate_proj():
        gates = _dot(hn(), wgates_ref[...]) + bgates_ref[...]
        lf = jnp.minimum(gates, 0.0) - jnp.log1p(jnp.exp(-jnp.abs(gates)))
        lf_hi = lf.astype(BF16)
        lf_lo = (lf - lf_hi.astype(F32)).astype(BF16)
        return gates, lf_hi, lf_lo

    def gate_cumsum(gates, lf_hi, lf_lo):
        tri = causal_mask().astype(BF16)
        bcum = _dot(tri, lf_hi) + _dot(tri, lf_lo)
        lane = lax.broadcasted_iota(jnp.int32, (TILE, LANES), 1)
        gcol = jnp.where(lane < N_HEADS, gates, bcum)
        return gcol, gcol.T

    def head_decay(h, gcol, grow):
        li_row = grow[h:h + 1, :]
        b_row = grow[N_HEADS + h:N_HEADS + h + 1, :]
        b_col = gcol[:, N_HEADS + h:N_HEADS + h + 1]
        m_st = m_ref[h, 0:1, 0:1]
        dmat = jnp.where(causal_mask(), b_col - b_row + li_row, -jnp.inf)
        a = b_col + m_st
        m_row = jnp.maximum(a, jnp.max(dmat, axis=1, keepdims=True))
        b_last = b_row[:, TILE - 1:TILE]
        g_row = b_last - b_row + li_row
        m_new = jnp.maximum(b_last + m_st, jnp.max(g_row, axis=1, keepdims=True))
        m_ref[h] = jnp.broadcast_to(m_new, (SUBLANES, LANES))
        return dict(
            dexp=jnp.exp(dmat - m_row),
            inter=jnp.exp(a - m_row),
            floor=jnp.exp(-m_row),
            w_row=jnp.exp(g_row - m_new),
            decay=jnp.exp(b_last + m_st - m_new),
        )

    def head_proj(h):
        q = _dot(hn(), wq_ref[:, h * DK:(h + 1) * DK])
        kt = lax.dot_general(wkt_ref[h * DK:(h + 1) * DK, :], hn(),
                             (((1,), (1,)), ((), ())), preferred_element_type=F32)
        v = _dot(hn(), wv_ref[:, h * DV:(h + 1) * DV]).astype(BF16)
        return q, kt, v

    def head_scores(q, kt):
        return _dot(q.astype(BF16), kt.astype(BF16))

    def head_num(h, q, v, s, dk):
        p32 = s * dk["dexp"]
        qs = (q * dk["inter"]).astype(BF16)
        num = _dot(jnp.concatenate([p32.astype(BF16), qs], axis=1),
                   jnp.concatenate([v, c_ref[h].astype(BF16)], axis=0))
        n_row = n_ref[h, 0:1, :]
        den = (jnp.sum(p32, axis=1, keepdims=True)
               + dk["inter"] * jnp.sum(q * n_row, axis=1, keepdims=True))
        return num / jnp.maximum(jnp.abs(den), dk["floor"])

    def head_ktw(h, kt, dk):
        ktw = kt * dk["w_row"]
        n_col = jnp.sum(ktw, axis=1, keepdims=True)
        n_add = jnp.broadcast_to(n_col, (DK, LANES)).T[0:SUBLANES, :]
        n_ref[h] = dk["decay"] * n_ref[h] + n_add
        return ktw.astype(BF16)

    def head_upd(h, ktw, v, dk):
        c_ref[h] = dk["decay"] * c_ref[h] + _dot(ktw, v)

    def head_gate(h):
        og = _dot(hn(), wog_ref[:, h * DV:(h + 1) * DV])
        zb = _dot(hn(), wzb_ref[:, h * DV:(h + 1) * DV])
        return _sigmoid(og) * _silu(zb)

    def head_out(h, hh, gate):
        yb_blk = (_rms(hh, ghead_ref[:, h * DV:(h + 1) * DV]) * gate).astype(BF16)
        contrib = _dot(yb_blk, wbout_ref[h * DV:(h + 1) * DV, :])
        if h == 0:
            yb_ref[...] = contrib
        else:
            yb_ref[...] += contrib

    def tail_gates():
        hnp = hnp_ref[...]
        return _dot(hnp, wgab_ref[:, 0:D_MODEL]), _dot(hnp, wgab_ref[:, D_MODEL:2 * D_MODEL])

    def tail_merge(ga, gb):
        return (_sigmoid(ga) * ya_ref[...] + _sigmoid(gb) * yb_ref[...]).astype(BF16)

    def tail_out_proj(merged):
        x1 = xp_ref[...] + _dot(merged, wo_ref[...])
        return x1, _rms(x1, gple_ref[...]).astype(BF16)

    def tail_finish(x1, r1):
        pe = _dot(pp_ref[...].astype(BF16), wple_ref[...])
        ple_gate = _sigmoid(_dot(r1, wpg_ref[...]))
        out_ref[...] = _rms(x1 + ple_gate * pe, gfinal_ref[...])

    @pl.when(step == pl.num_programs(0) - 1)
    def _():
        tail_finish(*tail_out_proj(tail_merge(*tail_gates())))

    @pl.when(step < pl.num_programs(0) - 1)
    def _():
        ga, gb = tail_gates()
        hn_ref[...] = _rms(x_ref[...], gmix_ref[...]).astype(BF16)
        gparts = gate_proj()
        conv0 = conv_in(0)
        x1, r1 = tail_out_proj(tail_merge(ga, gb))
        gcol, grow = gate_cumsum(*gparts)
        ya0 = conv_gate(0, conv0)
        dk0 = head_decay(0, gcol, grow)
        conv1 = conv_in(1)
        conv_out(0, ya0)
        dk1 = head_decay(1, gcol, grow)
        tail_finish(x1, r1)
        ya1 = conv_gate(1, conv1)

        q0, kt0, v0 = head_proj(0)
        conv_out(1, ya1)
        s0 = head_scores(q0, kt0)
        q1, kt1, v1 = head_proj(1)
        hh0 = head_num(0, q0, v0, s0, dk0)
        ktw0 = head_ktw(0, kt0, dk0)
        s1 = head_scores(q1, kt1)
        dk2 = head_decay(2, gcol, grow)
        gate0 = head_gate(0)
        q2, kt2, v2 = head_proj(2)
        head_out(0, hh0, gate0)
        hh1 = head_num(1, q1, v1, s1, dk1)
        ktw1 = head_ktw(1, kt1, dk1)
        s2 = head_scores(q2, kt2)
        dk3 = head_decay(3, gcol, grow)
        gate1 = head_gate(1)
        q3, kt3, v3 = head_proj(3)
        head_out(1, hh1, gate1)
        hh2 = head_num(2, q2, v2, s2, dk2)
        ktw2 = head_ktw(2, kt2, dk2)
        s3 = head_scores(q3, kt3)
        gate2 = head_gate(2)
        head_out(2, hh2, gate2)
        hh3 = head_num(3, q3, v3, s3, dk3)
        ktw3 = head_ktw(3, kt3, dk3)
        gate3 = head_gate(3)
        head_upd(0, ktw0, v0, dk0)
        head_upd(1, ktw1, v1, dk1)
        head_upd(2, ktw2, v2, dk2)
        head_out(3, hh3, gate3)
        head_upd(3, ktw3, v3, dk3)
        hnp_ref[...] = hn()


def _const_spec(shape):
    return pl.BlockSpec(shape, lambda s: (0,) * len(shape), pipeline_mode=pl.Buffered(1))


def kernel(x, p, g_mix, w_in, conv_w, conv_b, w_a_out, b_gates, g_head, w_b_out, w_o, g_ple,
           w_ple_gate, w_ple, g_final):
    bsz, seq, _ = x.shape
    assert w_in.shape[0] == 1 and seq % TILE == 0
    tiles_per_row = seq // TILE
    n_tiles = bsz * tiles_per_row
    assert w_in.shape[2] == 4 * CONV_WIDTH + 2 * QK_DIM + 3 * V_DIM + 2 * N_HEADS + 2 * D_MODEL
    wt = jnp.transpose(w_in[0])
    b_g = jnp.pad(b_gates[0], (0, LANES - 2 * N_HEADS)).reshape(1, LANES)

    hbm_weights = (wt, w_a_out[0], w_b_out[0], w_o[0], w_ple_gate[0], w_ple[0])
    consts = (
        g_mix[0].reshape(1, D_MODEL), b_g, conv_w[0], conv_b[0].reshape(1, CONV_WIDTH),
        g_head[0].reshape(1, V_DIM), g_ple[0].reshape(1, D_MODEL), g_final.reshape(1, D_MODEL),
    )

    def cur_map(s):
        c = jnp.minimum(s, n_tiles - 1)
        return (c // tiles_per_row, c % tiles_per_row, 0)

    def prev_map(s):
        c = jnp.maximum(s - 1, 0)
        return (c // tiles_per_row, c % tiles_per_row, 0)

    in_specs = [
        pl.BlockSpec((None, TILE, D_MODEL), cur_map),
        pl.BlockSpec((None, TILE, D_MODEL), prev_map),
        pl.BlockSpec((None, TILE, PLE_DIM), prev_map),
    ] + [pl.BlockSpec(memory_space=pl.ANY)] * len(hbm_weights) + [_const_spec(a.shape) for a in consts]

    return pl.pallas_call(
        functools.partial(_block_kernel, tiles_per_row),
        grid=(n_tiles + 1,),
        in_specs=in_specs,
        out_specs=pl.BlockSpec((None, TILE, D_MODEL), prev_map),
        out_shape=jax.ShapeDtypeStruct(x.shape, x.dtype),
        scratch_shapes=[
            pltpu.VMEM((N_HEADS, DK, DV), F32),
            pltpu.VMEM((N_HEADS, SUBLANES, DK), F32),
            pltpu.VMEM((N_HEADS, SUBLANES, LANES), F32),
            pltpu.VMEM((TILE + SUBLANES, CONV_WIDTH), F32),
            pltpu.VMEM((TILE, D_MODEL), BF16),
            pltpu.VMEM((TILE, D_MODEL), BF16),
            pltpu.VMEM((TILE, D_MODEL), F32),
            pltpu.VMEM((TILE, D_MODEL), F32),
            pltpu.VMEM((D_MODEL, 4 * CONV_WIDTH + QK_DIM + 3 * V_DIM + 2 * D_MODEL), BF16),
            pltpu.VMEM((QK_DIM, D_MODEL), BF16),
            pltpu.VMEM((D_MODEL, LANES), BF16),
            pltpu.VMEM((CONV_WIDTH, D_MODEL), BF16),
            pltpu.VMEM((V_DIM, D_MODEL), BF16),
            pltpu.VMEM((D_MODEL, D_MODEL), BF16),
            pltpu.VMEM((D_MODEL, D_MODEL), BF16),
            pltpu.VMEM((PLE_DIM, D_MODEL), BF16),
            pltpu.VMEM((2, W_KBLK, D_MODEL), F32),
            pltpu.VMEM((2 * N_HEADS, D_MODEL), F32),
            pltpu.SemaphoreType.DMA((4,)),
        ],
        compiler_params=pltpu.CompilerParams(
            dimension_semantics=("arbitrary",),
            vmem_limit_bytes=VMEM_LIMIT_BYTES,
        ),
        name="hybrid_block",
    )(x, x, p[0], *hbm_weights, *consts)
```

```python
import functools

import jax
import jax.numpy as jnp
from jax import lax
from jax.experimental import pallas as pl
from jax.experimental.pallas import tpu as pltpu

D_MODEL = 1024
PLE_DIM = 256
CONV_WIDTH = 1024
CONV_K = 3
N_HEADS = 4
QK_DIM = 1024
V_DIM = 2048
DK = QK_DIM // N_HEADS
DV = V_DIM // N_HEADS
EPS = 1e-6

LANES = 128
SUBLANES = 8
TILE = 256
CONV_BLK = 512
W_KBLK = 256
OGZ_PIECE0 = (4 * CONV_WIDTH + QK_DIM + V_DIM) // W_KBLK
VMEM_LIMIT_BYTES = 61 * 1024 * 1024

BF16 = jnp.bfloat16
F32 = jnp.float32


def _dot(a, b):
    return jnp.dot(a, b, preferred_element_type=F32)


def _sigmoid(v):
    return 0.5 * jnp.tanh(0.5 * v) + 0.5


def _silu(v):
    return v * _sigmoid(v)


def _rms(v, g):
    return v * lax.rsqrt(jnp.mean(v * v, axis=-1, keepdims=True) + EPS) * g


def _load_weights(wt_hbm, waout_hbm, wbout_hbm, wo_hbm, wpg_hbm, wple_hbm,
                  win_ref, wkt_ref, wgates_ref, waout_ref, wbout_ref, wo_ref, wpg_ref, wple_ref,
                  slots, gstage_ref, sem):
    depth = len(slots)
    eye = (lax.broadcasted_iota(jnp.int32, (W_KBLK, W_KBLK), 0)
           == lax.broadcasted_iota(jnp.int32, (W_KBLK, W_KBLK), 1)).astype(BF16)

    def copy(src_hbm, row, s):
        return pltpu.make_async_copy(src_hbm.at[pl.ds(row, W_KBLK), :], slots[s], sem.at[s])

    k_row0 = 4 * CONV_WIDTH + QK_DIM
    gate_row0 = k_row0 + QK_DIM + 3 * V_DIM
    n_pieces = win_ref.shape[1] // W_KBLK
    assert n_pieces % depth == 0

    def src_row(p):
        blk = CONV_WIDTH // W_KBLK
        c = (p // 8) * 2 + (p // 2) % 2
        conv = c + blk * ((p // 4) % 2) + 2 * blk * (p % 2)
        e = p - OGZ_PIECE0
        ogz = (4 * CONV_WIDTH + 2 * QK_DIM + V_DIM) // W_KBLK + 2 * (e // 4) + (e // 2) % 2 \
            + (V_DIM // W_KBLK) * (e % 2)
        plain = p + jnp.where(p >= k_row0 // W_KBLK, QK_DIM // W_KBLK, 0)
        piece = jnp.where(p < 4 * blk, conv, plain)
        piece = jnp.where(jnp.logical_and(p >= OGZ_PIECE0, p < OGZ_PIECE0 + 2 * V_DIM // W_KBLK),
                          ogz, piece)
        r = piece * W_KBLK
        return pl.multiple_of(r + jnp.where(r >= gate_row0, 2 * N_HEADS, 0), SUBLANES)

    for s in range(depth):
        copy(wt_hbm, src_row(s), s).start(priority=s % 2)

    def group(g, carry):
        for s in range(depth):
            p = g * depth + s
            copy(wt_hbm, src_row(p), s).wait()
            blk = slots[s][...]
            is_q = jnp.logical_and(p * W_KBLK >= 4 * CONV_WIDTH, p * W_KBLK < k_row0)
            blk = blk * jnp.where(is_q, DK ** -0.5, 1.0).astype(F32)
            blk_t = lax.dot_general(blk.astype(BF16), eye, (((0,), (0,)), ((), ())),
                                    preferred_element_type=F32)
            win_ref[:, pl.ds(pl.multiple_of(p * W_KBLK, W_KBLK), W_KBLK)] = blk_t.astype(BF16)

            @pl.when(p + depth < n_pieces)
            def _():
                copy(wt_hbm, src_row(p + depth), s).start(priority=s % 2)
        return carry

    lax.fori_loop(0, n_pieces // depth, group, 0)

    plain = [(wt_hbm, k_row0 + r, wkt_ref, r) for r in range(0, QK_DIM, W_KBLK)]
    for src_hbm, dst_ref in ((waout_hbm, waout_ref), (wbout_hbm, wbout_ref), (wo_hbm, wo_ref),
                             (wpg_hbm, wpg_ref), (wple_hbm, wple_ref)):
        assert dst_ref.shape[1] == D_MODEL and dst_ref.shape[0] % W_KBLK == 0
        plain += [(src_hbm, r, dst_ref, r) for r in range(0, dst_ref.shape[0], W_KBLK)]
    for i in range(min(depth, len(plain))):
        copy(plain[i][0], plain[i][1], i % depth).start(priority=i % 2)
    for i, (src_hbm, row, dst_ref, dst_row) in enumerate(plain):
        copy(src_hbm, row, i % depth).wait()
        dst_ref[dst_row:dst_row + W_KBLK, :] = slots[i % depth][...].astype(BF16)
        if i + depth < len(plain):
            nxt = plain[i + depth]
            copy(nxt[0], nxt[1], i % depth).start(priority=i % 2)

    gcopy = pltpu.make_async_copy(wt_hbm.at[pl.ds(gate_row0, 2 * N_HEADS), :], gstage_ref, sem.at[0])
    gcopy.start()
    gcopy.wait()
    gpad = jnp.concatenate([gstage_ref[...], jnp.zeros((LANES - 2 * N_HEADS, D_MODEL), F32)], axis=0)
    wgates_ref[...] = gpad.T.astype(BF16)


def _block_kernel(tiles_per_row,
                  x_ref, xp_ref, pp_ref, wt_hbm, waout_hbm, wbout_hbm, wo_hbm, wpg_hbm, wple_hbm,
                  gmix_ref, bgates_ref, convw_ref, convb_ref, ghead_ref, gple_ref, gfinal_ref,
                  out_ref,
                  c_ref, n_ref, m_ref, u_ref, hn_ref, hnp_ref, ya_ref, yb_ref,
                  win_ref, wkt_ref, wgates_ref, waout_ref, wbout_ref, wo_ref, wpg_ref, wple_ref,
                  stage_ref, gstage_ref, dma_sem):
    step = pl.program_id(0)

    o = 0
    wconv_ref = win_ref.at[:, o:o + 4 * CONV_WIDTH]; o += 4 * CONV_WIDTH
    wq_ref = win_ref.at[:, o:o + QK_DIM]; o += QK_DIM
    wv_ref = win_ref.at[:, o:o + V_DIM]; o += V_DIM
    wogz_ref = win_ref.at[:, o:o + 2 * V_DIM]; o += 2 * V_DIM
    wgab_ref = win_ref.at[:, o:o + 2 * D_MODEL]

    @pl.when(step == 0)
    def _():
        _load_weights(wt_hbm, waout_hbm, wbout_hbm, wo_hbm, wpg_hbm, wple_hbm,
                      win_ref, wkt_ref, wgates_ref, waout_ref, wbout_ref, wo_ref, wpg_ref, wple_ref,
                      [stage_ref.at[0], stage_ref.at[1], ya_ref, yb_ref], gstage_ref, dma_sem)
        for ref in (ya_ref, yb_ref, hnp_ref):
            ref[...] = jnp.zeros_like(ref)

    @pl.when(lax.rem(step, tiles_per_row) == 0)
    def _():
        c_ref[...] = jnp.zeros_like(c_ref)
        n_ref[...] = jnp.zeros_like(n_ref)
        m_ref[...] = jnp.full_like(m_ref, -jnp.inf)
        u_ref[0:SUBLANES, :] = jnp.zeros((SUBLANES, CONV_WIDTH), F32)

    def hn():
        return hn_ref[...]

    def pair_blocks(hn_w):
        n = hn_w.shape[1] // (2 * W_KBLK)
        a = [hn_w[:, (2 * j) * W_KBLK:(2 * j + 1) * W_KBLK] for j in range(n)]
        b = [hn_w[:, (2 * j + 1) * W_KBLK:(2 * j + 2) * W_KBLK] for j in range(n)]
        return a, b

    def conv_in(cb):
        c0 = cb * CONV_BLK
        cs = slice(c0, c0 + CONV_BLK)
        xa, ca = pair_blocks(_dot(hn(), wconv_ref[:, 4 * c0:4 * c0 + 2 * CONV_BLK]))
        u_ref[SUBLANES:SUBLANES + TILE, cs] = jnp.concatenate([x * c for x, c in zip(xa, ca)], axis=1)
        u0 = u_ref[SUBLANES:SUBLANES + TILE, cs]
        u1 = u_ref[SUBLANES - 1:SUBLANES - 1 + TILE, cs]
        u2 = u_ref[SUBLANES - 2:SUBLANES - 2 + TILE, cs]
        conv = (u2 * convw_ref[0:1, cs] + u1 * convw_ref[1:2, cs] + u0 * convw_ref[2:3, cs]
                + convb_ref[:, cs])
        u_ref[0:SUBLANES, cs] = u_ref[TILE:TILE + SUBLANES, cs]
        return conv

    def conv_gate(cb, conv):
        c0 = cb * CONV_BLK
        ba, za = pair_blocks(_dot(hn(), wconv_ref[:, 4 * c0 + 2 * CONV_BLK:4 * c0 + 4 * CONV_BLK]))
        gated = [b * _silu(z) for b, z in zip(ba, za)]
        return (jnp.concatenate(gated, axis=1) * conv).astype(BF16)

    def conv_out(cb, ya_blk):
        contrib = _dot(ya_blk, waout_ref[cb * CONV_BLK:(cb + 1) * CONV_BLK, :])
        if cb == 0:
            ya_ref[...] = contrib
        else:
            ya_ref[...] += contrib

    def causal_mask():
        return (lax.broadcasted_iota(jnp.int32, (TILE, TILE), 0)
                >= lax.broadcasted_iota(jnp.int32, (TILE, TILE), 1))

    def gate_proj():
        gates = _dot(hn(), wgates_ref[...]) + bgates_ref[...]
        lf = jnp.minimum(gates, 0.0) - jnp.log1p(jnp.exp(-jnp.abs(gates)))
        lf_hi = lf.astype(BF16)
        lf_lo = (lf - lf_hi.astype(F32)).astype(BF16)
        return gates, lf_hi, lf_lo

    def gate_cumsum(gates, lf_hi, lf_lo):
        tri = causal_mask().astype(BF16)
        bcum = _dot(tri, lf_hi) + _dot(tri, lf_lo)
        lane = lax.broadcasted_iota(jnp.int32, (TILE, LANES), 1)
        gcol = jnp.where(lane < N_HEADS, gates, bcum)
        return gcol, gcol.T

    def head_decay(h, gcol, grow):
        li_row = grow[h:h + 1, :]
        b_row = grow[N_HEADS + h:N_HEADS + h + 1, :]
        b_col = gcol[:, N_HEADS + h:N_HEADS + h + 1]
        m_st = m_ref[h, 0:1, 0:1]
        dmat = jnp.where(causal_mask(), b_col - b_row + li_row, -jnp.inf)
        a = b_col + m_st
        m_row = jnp.maximum(a, jnp.max(dmat, axis=1, keepdims=True))
        b_last = b_row[:, TILE - 1:TILE]
        g_row = b_last - b_row + li_row
        m_new = jnp.maximum(b_last + m_st, jnp.max(g_row, axis=1, keepdims=True))
        m_ref[h] = jnp.broadcast_to(m_new, (SUBLANES, LANES))
        return dict(
            dexp=jnp.exp(dmat - m_row),
            inter=jnp.exp(a - m_row),
            floor=jnp.exp(-m_row),
            w_row=jnp.exp(g_row - m_new),
            decay=jnp.exp(b_last + m_st - m_new),
        )

    def head_proj(h):
        q = _dot(hn(), wq_ref[:, h * DK:(h + 1) * DK])
        kt = lax.dot_general(wkt_ref[h * DK:(h + 1) * DK, :], hn(),
                             (((1,), (1,)), ((), ())), preferred_element_type=F32)
        v = _dot(hn(), wv_ref[:, h * DV:(h + 1) * DV]).astype(BF16)
        return q, kt, v

    def head_scores(q, kt):
        return _dot(q.astype(BF16), kt.astype(BF16))

    def head_num(h, q, v, s, dk):
        p32 = s * dk["dexp"]
        qs = (q * dk["inter"]).astype(BF16)
        num = _dot(jnp.concatenate([p32.astype(BF16), qs], axis=1),
                   jnp.concatenate([v, c_ref[h].astype(BF16)], axis=0))
        n_row = n_ref[h, 0:1, :]
        den = (jnp.sum(p32, axis=1, keepdims=True)
               + dk["inter"] * jnp.sum(q * n_row, axis=1, keepdims=True))
        return num / jnp.maximum(jnp.abs(den), dk["floor"])

    def head_ktw(h, kt, dk):
        ktw = kt * dk["w_row"]
        n_col = jnp.sum(ktw, axis=1, keepdims=True)
        n_add = jnp.broadcast_to(n_col, (DK, LANES)).T[0:SUBLANES, :]
        n_ref[h] = dk["decay"] * n_ref[h] + n_add
        return ktw.astype(BF16)

    def head_upd(h, ktw, v, dk):
        c_ref[h] = dk["decay"] * c_ref[h] + _dot(ktw, v)

    def head_gate(h):
        og, zb = pair_blocks(_dot(hn(), wogz_ref[:, 2 * h * DV:2 * (h + 1) * DV]))
        return jnp.concatenate([_sigmoid(o) * _silu(z) for o, z in zip(og, zb)], axis=1)

    def head_out(h, hh, gate):
        yb_blk = (_rms(hh, ghead_ref[:, h * DV:(h + 1) * DV]) * gate).astype(BF16)
        contrib = _dot(yb_blk, wbout_ref[h * DV:(h + 1) * DV, :])
        if h == 0:
            yb_ref[...] = contrib
        else:
            yb_ref[...] += contrib

    def tail_gates():
        hnp = hnp_ref[...]
        return _dot(hnp, wgab_ref[:, 0:D_MODEL]), _dot(hnp, wgab_ref[:, D_MODEL:2 * D_MODEL])

    def tail_merge(ga, gb):
        return (_sigmoid(ga) * ya_ref[...] + _sigmoid(gb) * yb_ref[...]).astype(BF16)

    def tail_out_proj(merged):
        x1 = xp_ref[...] + _dot(merged, wo_ref[...])
        return x1, _rms(x1, gple_ref[...]).astype(BF16)

    def tail_finish(x1, r1):
        pe = _dot(pp_ref[...].astype(BF16), wple_ref[...])
        ple_gate = _sigmoid(_dot(r1, wpg_ref[...]))
        out_ref[...] = _rms(x1 + ple_gate * pe, gfinal_ref[...])

    @pl.when(step == pl.num_programs(0) - 1)
    def _():
        tail_finish(*tail_out_proj(tail_merge(*tail_gates())))

    @pl.when(step < pl.num_programs(0) - 1)
    def _():
        ga, gb = tail_gates()
        hn_ref[...] = _rms(x_ref[...], gmix_ref[...]).astype(BF16)
        gparts = gate_proj()
        conv0 = conv_in(0)
        x1, r1 = tail_out_proj(tail_merge(ga, gb))
        gcol, grow = gate_cumsum(*gparts)
        ya0 = conv_gate(0, conv0)
        dk0 = head_decay(0, gcol, grow)
        conv1 = conv_in(1)
        conv_out(0, ya0)
        dk1 = head_decay(1, gcol, grow)
        tail_finish(x1, r1)
        ya1 = conv_gate(1, conv1)

        q0, kt0, v0 = head_proj(0)
        conv_out(1, ya1)
        s0 = head_scores(q0, kt0)
        q1, kt1, v1 = head_proj(1)
        hh0 = head_num(0, q0, v0, s0, dk0)
        ktw0 = head_ktw(0, kt0, dk0)
        s1 = head_scores(q1, kt1)
        dk2 = head_decay(2, gcol, grow)
        gate0 = head_gate(0)
        q2, kt2, v2 = head_proj(2)
        head_out(0, hh0, gate0)
        hh1 = head_num(1, q1, v1, s1, dk1)
        ktw1 = head_ktw(1, kt1, dk1)
        s2 = head_scores(q2, kt2)
        dk3 = head_decay(3, gcol, grow)
        gate1 = head_gate(1)
        q3, kt3, v3 = head_proj(3)
        head_out(1, hh1, gate1)
        hh2 = head_num(2, q2, v2, s2, dk2)
        ktw2 = head_ktw(2, kt2, dk2)
        s3 = head_scores(q3, kt3)
        gate2 = head_gate(2)
        head_out(2, hh2, gate2)
        hh3 = head_num(3, q3, v3, s3, dk3)
        ktw3 = head_ktw(3, kt3, dk3)
        gate3 = head_gate(3)
        head_upd(0, ktw0, v0, dk0)
        head_upd(1, ktw1, v1, dk1)
        head_upd(2, ktw2, v2, dk2)
        head_out(3, hh3, gate3)
        head_upd(3, ktw3, v3, dk3)
        hnp_ref[...] = hn()


def _const_spec(shape):
    return pl.BlockSpec(shape, lambda s: (0,) * len(shape), pipeline_mode=pl.Buffered(1))


def kernel(x, p, g_mix, w_in, conv_w, conv_b, w_a_out, b_gates, g_head, w_b_out, w_o, g_ple,
           w_ple_gate, w_ple, g_final):
    bsz, seq, _ = x.shape
    assert w_in.shape[0] == 1 and seq % TILE == 0
    tiles_per_row = seq // TILE
    n_tiles = bsz * tiles_per_row
    assert w_in.shape[2] == 4 * CONV_WIDTH + 2 * QK_DIM + 3 * V_DIM + 2 * N_HEADS + 2 * D_MODEL
    wt = jnp.transpose(w_in[0])
    b_g = jnp.pad(b_gates[0], (0, LANES - 2 * N_HEADS)).reshape(1, LANES)

    hbm_weights = (wt, w_a_out[0], w_b_out[0], w_o[0], w_ple_gate[0], w_ple[0])
    consts = (
        g_mix[0].reshape(1, D_MODEL), b_g, conv_w[0], conv_b[0].reshape(1, CONV_WIDTH),
        g_head[0].reshape(1, V_DIM), g_ple[0].reshape(1, D_MODEL), g_final.reshape(1, D_MODEL),
    )

    def cur_map(s):
        c = jnp.minimum(s, n_tiles - 1)
        return (c // tiles_per_row, c % tiles_per_row, 0)

    def prev_map(s):
        c = jnp.maximum(s - 1, 0)
        return (c // tiles_per_row, c % tiles_per_row, 0)

    in_specs = [
        pl.BlockSpec((None, TILE, D_MODEL), cur_map),
        pl.BlockSpec((None, TILE, D_MODEL), prev_map),
        pl.BlockSpec((None, TILE, PLE_DIM), prev_map),
    ] + [pl.BlockSpec(memory_space=pl.ANY)] * len(hbm_weights) + [_const_spec(a.shape) for a in consts]

    return pl.pallas_call(
        functools.partial(_block_kernel, tiles_per_row),
        grid=(n_tiles + 1,),
        in_specs=in_specs,
        out_specs=pl.BlockSpec((None, TILE, D_MODEL), prev_map),
        out_shape=jax.ShapeDtypeStruct(x.shape, x.dtype),
        scratch_shapes=[
            pltpu.VMEM((N_HEADS, DK, DV), F32),
            pltpu.VMEM((N_HEADS, SUBLANES, DK), F32),
            pltpu.VMEM((N_HEADS, SUBLANES, LANES), F32),
            pltpu.VMEM((TILE + SUBLANES, CONV_WIDTH), F32),
            pltpu.VMEM((TILE, D_MODEL), BF16),
            pltpu.VMEM((TILE, D_MODEL), BF16),
            pltpu.VMEM((TILE, D_MODEL), F32),
            pltpu.VMEM((TILE, D_MODEL), F32),
            pltpu.VMEM((D_MODEL, 4 * CONV_WIDTH + QK_DIM + 3 * V_DIM + 2 * D_MODEL), BF16),
            pltpu.VMEM((QK_DIM, D_MODEL), BF16),
            pltpu.VMEM((D_MODEL, LANES), BF16),
            pltpu.VMEM((CONV_WIDTH, D_MODEL), BF16),
            pltpu.VMEM((V_DIM, D_MODEL), BF16),
            pltpu.VMEM((D_MODEL, D_MODEL), BF16),
            pltpu.VMEM((D_MODEL, D_MODEL), BF16),
            pltpu.VMEM((PLE_DIM, D_MODEL), BF16),
            pltpu.VMEM((2, W_KBLK, D_MODEL), F32),
            pltpu.VMEM((2 * N_HEADS, D_MODEL), F32),
            pltpu.SemaphoreType.DMA((4,)),
        ],
        compiler_params=pltpu.CompilerParams(
            dimension_semantics=("arbitrary",),
            vmem_limit_bytes=VMEM_LIMIT_BYTES,
        ),
        name="hybrid_block",
    )(x, x, p[0], *hbm_weights, *consts)
```

```python
import functools

import jax
import jax.numpy as jnp
from jax import lax
from jax.experimental import pallas as pl
from jax.experimental.pallas import tpu as pltpu

D_MODEL = 1024
PLE_DIM = 256
CONV_WIDTH = 1024
CONV_K = 3
N_HEADS = 4
QK_DIM = 1024
V_DIM = 2048
DK = QK_DIM // N_HEADS
DV = V_DIM // N_HEADS
EPS = 1e-6

LANES = 128
SUBLANES = 8
TILE = 256
CONV_BLK = 512
W_KBLK = 256
OGZ_PIECE0 = (4 * CONV_WIDTH + QK_DIM + V_DIM) // W_KBLK
GAB_PIECE0 = OGZ_PIECE0 + 2 * V_DIM // W_KBLK
VMEM_LIMIT_BYTES = 61 * 1024 * 1024

BF16 = jnp.bfloat16
F32 = jnp.float32


def _dot(a, b):
    return jnp.dot(a, b, preferred_element_type=F32)


def _sigmoid(v):
    return 0.5 * jnp.tanh(0.5 * v) + 0.5


def _silu(v):
    return v * _sigmoid(v)


def _rms(v, g):
    return v * lax.rsqrt(jnp.mean(v * v, axis=-1, keepdims=True) + EPS) * g


def _load_weights(wt_hbm, waout_hbm, wbout_hbm, wo_hbm, wpg_hbm, wple_hbm,
                  win_ref, wkt_ref, wgates_ref, waout_ref, wbout_ref, wo_ref, wpg_ref, wple_ref,
                  slots, gstage_ref, sem):
    depth = len(slots)
    eye = (lax.broadcasted_iota(jnp.int32, (W_KBLK, W_KBLK), 0)
           == lax.broadcasted_iota(jnp.int32, (W_KBLK, W_KBLK), 1)).astype(BF16)

    def copy(src_hbm, row, s):
        return pltpu.make_async_copy(src_hbm.at[pl.ds(row, W_KBLK), :], slots[s], sem.at[s])

    k_row0 = 4 * CONV_WIDTH + QK_DIM
    gate_row0 = k_row0 + QK_DIM + 3 * V_DIM
    n_pieces = win_ref.shape[1] // W_KBLK
    assert n_pieces % depth == 0

    def src_row(p):
        blk = CONV_WIDTH // W_KBLK
        c = (p // 8) * 2 + (p // 2) % 2
        conv = c + blk * ((p // 4) % 2) + 2 * blk * (p % 2)
        e = p - OGZ_PIECE0
        ogz = (4 * CONV_WIDTH + 2 * QK_DIM + V_DIM) // W_KBLK + 2 * (e // 4) + (e // 2) % 2 \
            + (V_DIM // W_KBLK) * (e % 2)
        m = p - GAB_PIECE0
        pm = jnp.where(p >= GAB_PIECE0, GAB_PIECE0 + m // 2 + (D_MODEL // W_KBLK) * (m % 2), p)
        plain = pm + jnp.where(pm >= k_row0 // W_KBLK, QK_DIM // W_KBLK, 0)
        piece = jnp.where(p < 4 * blk, conv, plain)
        piece = jnp.where(jnp.logical_and(p >= OGZ_PIECE0, p < OGZ_PIECE0 + 2 * V_DIM // W_KBLK),
                          ogz, piece)
        r = piece * W_KBLK
        return pl.multiple_of(r + jnp.where(r >= gate_row0, 2 * N_HEADS, 0), SUBLANES)

    for s in range(depth):
        copy(wt_hbm, src_row(s), s).start(priority=s % 2)

    def group(g, carry):
        for s in range(depth):
            p = g * depth + s
            copy(wt_hbm, src_row(p), s).wait()
            blk = slots[s][...]
            is_q = jnp.logical_and(p * W_KBLK >= 4 * CONV_WIDTH, p * W_KBLK < k_row0)
            blk = blk * jnp.where(is_q, DK ** -0.5, 1.0).astype(F32)
            blk_t = lax.dot_general(blk.astype(BF16), eye, (((0,), (0,)), ((), ())),
                                    preferred_element_type=F32)
            win_ref[:, pl.ds(pl.multiple_of(p * W_KBLK, W_KBLK), W_KBLK)] = blk_t.astype(BF16)

            @pl.when(p + depth < n_pieces)
            def _():
                copy(wt_hbm, src_row(p + depth), s).start(priority=s % 2)
        return carry

    lax.fori_loop(0, n_pieces // depth, group, 0)

    plain = [(wt_hbm, k_row0 + r, wkt_ref, r) for r in range(0, QK_DIM, W_KBLK)]
    for src_hbm, dst_ref in ((waout_hbm, waout_ref), (wbout_hbm, wbout_ref), (wo_hbm, wo_ref),
                             (wpg_hbm, wpg_ref), (wple_hbm, wple_ref)):
        assert dst_ref.shape[1] == D_MODEL and dst_ref.shape[0] % W_KBLK == 0
        plain += [(src_hbm, r, dst_ref, r) for r in range(0, dst_ref.shape[0], W_KBLK)]
    for i in range(min(depth, len(plain))):
        copy(plain[i][0], plain[i][1], i % depth).start(priority=i % 2)
    for i, (src_hbm, row, dst_ref, dst_row) in enumerate(plain):
        copy(src_hbm, row, i % depth).wait()
        dst_ref[dst_row:dst_row + W_KBLK, :] = slots[i % depth][...].astype(BF16)
        if i + depth < len(plain):
            nxt = plain[i + depth]
            copy(nxt[0], nxt[1], i % depth).start(priority=i % 2)

    gcopy = pltpu.make_async_copy(wt_hbm.at[pl.ds(gate_row0, 2 * N_HEADS), :], gstage_ref, sem.at[0])
    gcopy.start()
    gcopy.wait()
    gpad = jnp.concatenate([gstage_ref[...], jnp.zeros((LANES - 2 * N_HEADS, D_MODEL), F32)], axis=0)
    wgates_ref[...] = gpad.T.astype(BF16)


def _block_kernel(tiles_per_row,
                  x_ref, xp_ref, pp_ref, wt_hbm, waout_hbm, wbout_hbm, wo_hbm, wpg_hbm, wple_hbm,
                  gmix_ref, bgates_ref, convw_ref, convb_ref, ghead_ref, gple_ref, gfinal_ref,
                  out_ref,
                  c_ref, n_ref, m_ref, u_ref, hn_ref, hnp_ref, ya_ref, yb_ref,
                  win_ref, wkt_ref, wgates_ref, waout_ref, wbout_ref, wo_ref, wpg_ref, wple_ref,
                  stage_ref, gstage_ref, dma_sem):
    step = pl.program_id(0)

    o = 0
    wconv_ref = win_ref.at[:, o:o + 4 * CONV_WIDTH]; o += 4 * CONV_WIDTH
    wq_ref = win_ref.at[:, o:o + QK_DIM]; o += QK_DIM
    wv_ref = win_ref.at[:, o:o + V_DIM]; o += V_DIM
    wogz_ref = win_ref.at[:, o:o + 2 * V_DIM]; o += 2 * V_DIM
    wgab_ref = win_ref.at[:, o:o + 2 * D_MODEL]

    @pl.when(step == 0)
    def _():
        _load_weights(wt_hbm, waout_hbm, wbout_hbm, wo_hbm, wpg_hbm, wple_hbm,
                      win_ref, wkt_ref, wgates_ref, waout_ref, wbout_ref, wo_ref, wpg_ref, wple_ref,
                      [stage_ref.at[0], stage_ref.at[1], ya_ref, yb_ref], gstage_ref, dma_sem)
        for ref in (ya_ref, yb_ref, hnp_ref):
            ref[...] = jnp.zeros_like(ref)

    @pl.when(lax.rem(step, tiles_per_row) == 0)
    def _():
        c_ref[...] = jnp.zeros_like(c_ref)
        n_ref[...] = jnp.zeros_like(n_ref)
        m_ref[...] = jnp.full_like(m_ref, -jnp.inf)
        u_ref[0:SUBLANES, :] = jnp.zeros((SUBLANES, CONV_WIDTH), F32)

    def hn():
        return hn_ref[...]

    def pair_blocks(hn_w):
        n = hn_w.shape[1] // (2 * W_KBLK)
        a = [hn_w[:, (2 * j) * W_KBLK:(2 * j + 1) * W_KBLK] for j in range(n)]
        b = [hn_w[:, (2 * j + 1) * W_KBLK:(2 * j + 2) * W_KBLK] for j in range(n)]
        return a, b

    def conv_in(cb):
        c0 = cb * CONV_BLK
        cs = slice(c0, c0 + CONV_BLK)
        xa, ca = pair_blocks(_dot(hn(), wconv_ref[:, 4 * c0:4 * c0 + 2 * CONV_BLK]))
        u_ref[SUBLANES:SUBLANES + TILE, cs] = jnp.concatenate([x * c for x, c in zip(xa, ca)], axis=1)
        u0 = u_ref[SUBLANES:SUBLANES + TILE, cs]
        u1 = u_ref[SUBLANES - 1:SUBLANES - 1 + TILE, cs]
        u2 = u_ref[SUBLANES - 2:SUBLANES - 2 + TILE, cs]
        conv = (u2 * convw_ref[0:1, cs] + u1 * convw_ref[1:2, cs] + u0 * convw_ref[2:3, cs]
                + convb_ref[:, cs])
        u_ref[0:SUBLANES, cs] = u_ref[TILE:TILE + SUBLANES, cs]
        return conv

    def conv_gate(cb, conv):
        c0 = cb * CONV_BLK
        ba, za = pair_blocks(_dot(hn(), wconv_ref[:, 4 * c0 + 2 * CONV_BLK:4 * c0 + 4 * CONV_BLK]))
        gated = [b * _silu(z) for b, z in zip(ba, za)]
        return (jnp.concatenate(gated, axis=1) * conv).astype(BF16)

    def conv_out(ya_blks):
        ya_ref[...] = _dot(jnp.concatenate(ya_blks, axis=1), waout_ref[...])

    def causal_mask():
        return (lax.broadcasted_iota(jnp.int32, (TILE, TILE), 0)
                >= lax.broadcasted_iota(jnp.int32, (TILE, TILE), 1))

    def gate_proj():
        gates = _dot(hn(), wgates_ref[...]) + bgates_ref[...]
        lf = jnp.minimum(gates, 0.0) - jnp.log1p(jnp.exp(-jnp.abs(gates)))
        lf_hi = lf.astype(BF16)
        lf_lo = (lf - lf_hi.astype(F32)).astype(BF16)
        return gates, lf_hi, lf_lo

    def gate_cumsum(gates, lf_hi, lf_lo):
        tri = causal_mask().astype(BF16)
        bcum = _dot(tri, lf_hi) + _dot(tri, lf_lo)
        lane = lax.broadcasted_iota(jnp.int32, (TILE, LANES), 1)
        gcol = jnp.where(lane < N_HEADS, gates, bcum)
        return gcol, gcol.T

    def head_decay(h, gcol, grow):
        li_row = grow[h:h + 1, :]
        b_row = grow[N_HEADS + h:N_HEADS + h + 1, :]
        b_col = gcol[:, N_HEADS + h:N_HEADS + h + 1]
        m_st = m_ref[h, 0:1, 0:1]
        dmat = jnp.where(causal_mask(), b_col - b_row + li_row, -jnp.inf)
        a = b_col + m_st
        m_row = jnp.maximum(a, jnp.max(dmat, axis=1, keepdims=True))
        b_last = b_row[:, TILE - 1:TILE]
        g_row = b_last - b_row + li_row
        m_new = jnp.maximum(b_last + m_st, jnp.max(g_row, axis=1, keepdims=True))
        m_ref[h] = jnp.broadcast_to(m_new, (SUBLANES, LANES))
        return dict(
            dexp=jnp.exp(dmat - m_row),
            inter=jnp.exp(a - m_row),
            floor=jnp.exp(-m_row),
            w_row=jnp.exp(g_row - m_new),
            decay=jnp.exp(b_last + m_st - m_new),
        )

    def head_proj(h):
        q = _dot(hn(), wq_ref[:, h * DK:(h + 1) * DK])
        kt = lax.dot_general(wkt_ref[h * DK:(h + 1) * DK, :], hn(),
                             (((1,), (1,)), ((), ())), preferred_element_type=F32)
        v = _dot(hn(), wv_ref[:, h * DV:(h + 1) * DV]).astype(BF16)
        return q, kt, v

    def head_scores(q, kt):
        return _dot(q.astype(BF16), kt.astype(BF16))

    def head_num(h, q, v, s, dk):
        p32 = s * dk["dexp"]
        qs = (q * dk["inter"]).astype(BF16)
        num = _dot(jnp.concatenate([p32.astype(BF16), qs], axis=1),
                   jnp.concatenate([v, c_ref[h].astype(BF16)], axis=0))
        n_row = n_ref[h, 0:1, :]
        den = (jnp.sum(p32, axis=1, keepdims=True)
               + dk["inter"] * jnp.sum(q * n_row, axis=1, keepdims=True))
        return num / jnp.maximum(jnp.abs(den), dk["floor"])

    def head_ktw(h, kt, dk):
        ktw = kt * dk["w_row"]
        n_col = jnp.sum(ktw, axis=1, keepdims=True)
        n_add = jnp.broadcast_to(n_col, (DK, LANES)).T[0:SUBLANES, :]
        n_ref[h] = dk["decay"] * n_ref[h] + n_add
        return ktw.astype(BF16)

    def head_upd(h, ktw, v, dk):
        c_ref[h] = dk["decay"] * c_ref[h] + _dot(ktw, v)

    def head_gate(h):
        og, zb = pair_blocks(_dot(hn(), wogz_ref[:, 2 * h * DV:2 * (h + 1) * DV]))
        return jnp.concatenate([_sigmoid(o) * _silu(z) for o, z in zip(og, zb)], axis=1)

    def head_y(h, hh, gate):
        return (_rms(hh, ghead_ref[:, h * DV:(h + 1) * DV]) * gate).astype(BF16)

    def heads_out(h0, yb_blks):
        contrib = _dot(jnp.concatenate(yb_blks, axis=1),
                       wbout_ref[h0 * DV:(h0 + len(yb_blks)) * DV, :])
        if h0 == 0:
            yb_ref[...] = contrib
        else:
            yb_ref[...] += contrib

    def tail_gates():
        return pair_blocks(_dot(hnp_ref[...], wgab_ref[...]))

    def tail_merge(ga, gb):
        merged = [_sigmoid(a) * ya_ref[:, j * W_KBLK:(j + 1) * W_KBLK]
                  + _sigmoid(b) * yb_ref[:, j * W_KBLK:(j + 1) * W_KBLK]
                  for j, (a, b) in enumerate(zip(ga, gb))]
        return jnp.concatenate(merged, axis=1).astype(BF16)

    def tail_out_proj(merged):
        x1 = xp_ref[...] + _dot(merged, wo_ref[...])
        return x1, _rms(x1, gple_ref[...]).astype(BF16)

    def tail_finish(x1, r1):
        pe = _dot(pp_ref[...].astype(BF16), wple_ref[...])
        ple_gate = _sigmoid(_dot(r1, wpg_ref[...]))
        out_ref[...] = _rms(x1 + ple_gate * pe, gfinal_ref[...])

    @pl.when(step == pl.num_programs(0) - 1)
    def _():
        tail_finish(*tail_out_proj(tail_merge(*tail_gates())))

    @pl.when(step < pl.num_programs(0) - 1)
    def _():
        ga, gb = tail_gates()
        hn_ref[...] = _rms(x_ref[...], gmix_ref[...]).astype(BF16)
        gparts = gate_proj()
        conv0 = conv_in(0)
        x1, r1 = tail_out_proj(tail_merge(ga, gb))
        gcol, grow = gate_cumsum(*gparts)
        ya0 = conv_gate(0, conv0)
        dk0 = head_decay(0, gcol, grow)
        conv1 = conv_in(1)
        dk1 = head_decay(1, gcol, grow)
        tail_finish(x1, r1)
        ya1 = conv_gate(1, conv1)

        q0, kt0, v0 = head_proj(0)
        conv_out([ya0, ya1])
        s0 = head_scores(q0, kt0)
        q1, kt1, v1 = head_proj(1)
        hh0 = head_num(0, q0, v0, s0, dk0)
        ktw0 = head_ktw(0, kt0, dk0)
        s1 = head_scores(q1, kt1)
        dk2 = head_decay(2, gcol, grow)
        gate0 = head_gate(0)
        q2, kt2, v2 = head_proj(2)
        yb0 = head_y(0, hh0, gate0)
        hh1 = head_num(1, q1, v1, s1, dk1)
        ktw1 = head_ktw(1, kt1, dk1)
        s2 = head_scores(q2, kt2)
        dk3 = head_decay(3, gcol, grow)
        gate1 = head_gate(1)
        q3, kt3, v3 = head_proj(3)
        heads_out(0, [yb0, head_y(1, hh1, gate1)])
        hh2 = head_num(2, q2, v2, s2, dk2)
        ktw2 = head_ktw(2, kt2, dk2)
        s3 = head_scores(q3, kt3)
        gate2 = head_gate(2)
        yb2 = head_y(2, hh2, gate2)
        hh3 = head_num(3, q3, v3, s3, dk3)
        ktw3 = head_ktw(3, kt3, dk3)
        gate3 = head_gate(3)
        head_upd(0, ktw0, v0, dk0)
        head_upd(1, ktw1, v1, dk1)
        head_upd(2, ktw2, v2, dk2)
        heads_out(2, [yb2, head_y(3, hh3, gate3)])
        head_upd(3, ktw3, v3, dk3)
        hnp_ref[...] = hn()


def _const_spec(shape):
    return pl.BlockSpec(shape, lambda s: (0,) * len(shape), pipeline_mode=pl.Buffered(1))


def kernel(x, p, g_mix, w_in, conv_w, conv_b, w_a_out, b_gates, g_head, w_b_out, w_o, g_ple,
           w_ple_gate, w_ple, g_final):
    bsz, seq, _ = x.shape
    assert w_in.shape[0] == 1 and seq % TILE == 0
    tiles_per_row = seq // TILE
    n_tiles = bsz * tiles_per_row
    assert w_in.shape[2] == 4 * CONV_WIDTH + 2 * QK_DIM + 3 * V_DIM + 2 * N_HEADS + 2 * D_MODEL
    wt = jnp.transpose(w_in[0])
    b_g = jnp.pad(b_gates[0], (0, LANES - 2 * N_HEADS)).reshape(1, LANES)

    hbm_weights = (wt, w_a_out[0], w_b_out[0], w_o[0], w_ple_gate[0], w_ple[0])
    consts = (
        g_mix[0].reshape(1, D_MODEL), b_g, conv_w[0], conv_b[0].reshape(1, CONV_WIDTH),
        g_head[0].reshape(1, V_DIM), g_ple[0].reshape(1, D_MODEL), g_final.reshape(1, D_MODEL),
    )

    def cur_map(s):
        c = jnp.minimum(s, n_tiles - 1)
        return (c // tiles_per_row, c % tiles_per_row, 0)

    def prev_map(s):
        c = jnp.maximum(s - 1, 0)
        return (c // tiles_per_row, c % tiles_per_row, 0)

    in_specs = [
        pl.BlockSpec((None, TILE, D_MODEL), cur_map),
        pl.BlockSpec((None, TILE, D_MODEL), prev_map),
        pl.BlockSpec((None, TILE, PLE_DIM), prev_map),
    ] + [pl.BlockSpec(memory_space=pl.ANY)] * len(hbm_weights) + [_const_spec(a.shape) for a in consts]

    return pl.pallas_call(
        functools.partial(_block_kernel, tiles_per_row),
        grid=(n_tiles + 1,),
        in_specs=in_specs,
        out_specs=pl.BlockSpec((None, TILE, D_MODEL), prev_map),
        out_shape=jax.ShapeDtypeStruct(x.shape, x.dtype),
        scratch_shapes=[
            pltpu.VMEM((N_HEADS, DK, DV), F32),
            pltpu.VMEM((N_HEADS, SUBLANES, DK), F32),
            pltpu.VMEM((N_HEADS, SUBLANES, LANES), F32),
            pltpu.VMEM((TILE + SUBLANES, CONV_WIDTH), F32),
            pltpu.VMEM((TILE, D_MODEL), BF16),
            pltpu.VMEM((TILE, D_MODEL), BF16),
            pltpu.VMEM((TILE, D_MODEL), F32),
            pltpu.VMEM((TILE, D_MODEL), F32),
            pltpu.VMEM((D_MODEL, 4 * CONV_WIDTH + QK_DIM + 3 * V_DIM + 2 * D_MODEL), BF16),
            pltpu.VMEM((QK_DIM, D_MODEL), BF16),
            pltpu.VMEM((D_MODEL, LANES), BF16),
            pltpu.VMEM((CONV_WIDTH, D_MODEL), BF16),
            pltpu.VMEM((V_DIM, D_MODEL), BF16),
            pltpu.VMEM((D_MODEL, D_MODEL), BF16),
            pltpu.VMEM((D_MODEL, D_MODEL), BF16),
            pltpu.VMEM((PLE_DIM, D_MODEL), BF16),
            pltpu.VMEM((2, W_KBLK, D_MODEL), F32),
            pltpu.VMEM((2 * N_HEADS, D_MODEL), F32),
            pltpu.SemaphoreType.DMA((4,)),
        ],
        compiler_params=pltpu.CompilerParams(
            dimension_semantics=("arbitrary",),
            vmem_limit_bytes=VMEM_LIMIT_BYTES,
        ),
        name="hybrid_block",
    )(x, x, p[0], *hbm_weights, *consts)
```

```python
import functools

import jax
import jax.numpy as jnp
from jax import lax
from jax.experimental import pallas as pl
from jax.experimental.pallas import tpu as pltpu

D_MODEL = 1024
PLE_DIM = 256
CONV_WIDTH = 1024
CONV_K = 3
N_HEADS = 4
QK_DIM = 1024
V_DIM = 2048
DK = QK_DIM // N_HEADS
DV = V_DIM // N_HEADS
EPS = 1e-6

LANES = 128
SUBLANES = 8
TILE = 256
CONV_BLK = 512
W_KBLK = 256
OGZ_PIECE0 = (4 * CONV_WIDTH + QK_DIM + V_DIM) // W_KBLK
GAB_PIECE0 = OGZ_PIECE0 + 2 * V_DIM // W_KBLK
VMEM_LIMIT_BYTES = 61 * 1024 * 1024

BF16 = jnp.bfloat16
F32 = jnp.float32


def _dot(a, b):
    return jnp.dot(a, b, preferred_element_type=F32)


def _sigmoid(v):
    return 0.5 * jnp.tanh(0.5 * v) + 0.5


def _silu(v):
    return v * _sigmoid(v)


def _rms(v, g):
    return v * lax.rsqrt(jnp.mean(v * v, axis=-1, keepdims=True) + EPS) * g


def _load_weights(wt_hbm, waout_hbm, wbout_hbm, wo_hbm, wpg_hbm, wple_hbm,
                  win_ref, wkt_ref, wgates_ref, waout_ref, wbout_ref, wo_ref, wpg_ref, wple_ref,
                  slots, gstage_ref, sem):
    depth = len(slots)
    eye = (lax.broadcasted_iota(jnp.int32, (W_KBLK, W_KBLK), 0)
           == lax.broadcasted_iota(jnp.int32, (W_KBLK, W_KBLK), 1)).astype(BF16)

    def copy(src_hbm, row, s):
        return pltpu.make_async_copy(src_hbm.at[pl.ds(row, W_KBLK), :], slots[s], sem.at[s])

    k_row0 = 4 * CONV_WIDTH + QK_DIM
    gate_row0 = k_row0 + QK_DIM + 3 * V_DIM
    n_pieces = win_ref.shape[1] // W_KBLK
    assert n_pieces % depth == 0

    def src_row(p):
        blk = CONV_WIDTH // W_KBLK
        c = (p // 8) * 2 + (p // 2) % 2
        conv = c + blk * ((p // 4) % 2) + 2 * blk * (p % 2)
        e = p - OGZ_PIECE0
        ogz = (4 * CONV_WIDTH + 2 * QK_DIM + V_DIM) // W_KBLK + 2 * (e // 4) + (e // 2) % 2 \
            + (V_DIM // W_KBLK) * (e % 2)
        m = p - GAB_PIECE0
        pm = jnp.where(p >= GAB_PIECE0, GAB_PIECE0 + m // 2 + (D_MODEL // W_KBLK) * (m % 2), p)
        plain = pm + jnp.where(pm >= k_row0 // W_KBLK, QK_DIM // W_KBLK, 0)
        piece = jnp.where(p < 4 * blk, conv, plain)
        piece = jnp.where(jnp.logical_and(p >= OGZ_PIECE0, p < OGZ_PIECE0 + 2 * V_DIM // W_KBLK),
                          ogz, piece)
        r = piece * W_KBLK
        return pl.multiple_of(r + jnp.where(r >= gate_row0, 2 * N_HEADS, 0), SUBLANES)

    for s in range(depth):
        copy(wt_hbm, src_row(s), s).start(priority=s % 2)

    def group(g, carry):
        for s in range(depth):
            p = g * depth + s
            copy(wt_hbm, src_row(p), s).wait()
            blk = slots[s][...]
            is_q = jnp.logical_and(p * W_KBLK >= 4 * CONV_WIDTH, p * W_KBLK < k_row0)
            blk = blk * jnp.where(is_q, DK ** -0.5, 1.0).astype(F32)
            blk_t = lax.dot_general(blk.astype(BF16), eye, (((0,), (0,)), ((), ())),
                                    preferred_element_type=F32)
            win_ref[:, pl.ds(pl.multiple_of(p * W_KBLK, W_KBLK), W_KBLK)] = blk_t.astype(BF16)

            @pl.when(p + depth < n_pieces)
            def _():
                copy(wt_hbm, src_row(p + depth), s).start(priority=s % 2)
        return carry

    lax.fori_loop(0, n_pieces // depth, group, 0)

    plain = [(wt_hbm, k_row0 + r, wkt_ref, r) for r in range(0, QK_DIM, W_KBLK)]
    for src_hbm, dst_ref in ((waout_hbm, waout_ref), (wbout_hbm, wbout_ref), (wo_hbm, wo_ref),
                             (wpg_hbm, wpg_ref), (wple_hbm, wple_ref)):
        assert dst_ref.shape[1] == D_MODEL and dst_ref.shape[0] % W_KBLK == 0
        plain += [(src_hbm, r, dst_ref, r) for r in range(0, dst_ref.shape[0], W_KBLK)]
    for i in range(min(depth, len(plain))):
        copy(plain[i][0], plain[i][1], i % depth).start(priority=i % 2)
    for i, (src_hbm, row, dst_ref, dst_row) in enumerate(plain):
        copy(src_hbm, row, i % depth).wait()
        dst_ref[dst_row:dst_row + W_KBLK, :] = slots[i % depth][...].astype(BF16)
        if i + depth < len(plain):
            nxt = plain[i + depth]
            copy(nxt[0], nxt[1], i % depth).start(priority=i % 2)

    gcopy = pltpu.make_async_copy(wt_hbm.at[pl.ds(gate_row0, 2 * N_HEADS), :], gstage_ref, sem.at[0])
    gcopy.start()
    gcopy.wait()
    gpad = jnp.concatenate([gstage_ref[...], jnp.zeros((LANES - 2 * N_HEADS, D_MODEL), F32)], axis=0)
    wgates_ref[...] = gpad.T.astype(BF16)


def _block_kernel(tiles_per_row,
                  x_ref, xp_ref, pp_ref, wt_hbm, waout_hbm, wbout_hbm, wo_hbm, wpg_hbm, wple_hbm,
                  gmix_ref, bgates_ref, convw_ref, convb_ref, ghead_ref, gple_ref, gfinal_ref,
                  out_ref,
                  c_ref, n_ref, m_ref, u_ref, hn_ref, hnp_ref, ya_ref, yb_ref,
                  win_ref, wkt_ref, wgates_ref, waout_ref, wbout_ref, wo_ref, wpg_ref, wple_ref,
                  stage_ref, gstage_ref, dma_sem):
    step = pl.program_id(0)

    o = 0
    wconv_ref = win_ref.at[:, o:o + 4 * CONV_WIDTH]; o += 4 * CONV_WIDTH
    wq_ref = win_ref.at[:, o:o + QK_DIM]; o += QK_DIM
    wv_ref = win_ref.at[:, o:o + V_DIM]; o += V_DIM
    wogz_ref = win_ref.at[:, o:o + 2 * V_DIM]; o += 2 * V_DIM
    wgab_ref = win_ref.at[:, o:o + 2 * D_MODEL]

    @pl.when(step == 0)
    def _():
        _load_weights(wt_hbm, waout_hbm, wbout_hbm, wo_hbm, wpg_hbm, wple_hbm,
                      win_ref, wkt_ref, wgates_ref, waout_ref, wbout_ref, wo_ref, wpg_ref, wple_ref,
                      [stage_ref.at[0], stage_ref.at[1], ya_ref, yb_ref], gstage_ref, dma_sem)
        for ref in (ya_ref, yb_ref, hnp_ref):
            ref[...] = jnp.zeros_like(ref)

    @pl.when(lax.rem(step, tiles_per_row) == 0)
    def _():
        c_ref[...] = jnp.zeros_like(c_ref)
        n_ref[...] = jnp.zeros_like(n_ref)
        m_ref[...] = jnp.full_like(m_ref, -jnp.inf)
        u_ref[0:SUBLANES, :] = jnp.zeros((SUBLANES, CONV_WIDTH), F32)

    def hn():
        return hn_ref[...]

    def pair_blocks(hn_w):
        n = hn_w.shape[1] // (2 * W_KBLK)
        a = [hn_w[:, (2 * j) * W_KBLK:(2 * j + 1) * W_KBLK] for j in range(n)]
        b = [hn_w[:, (2 * j + 1) * W_KBLK:(2 * j + 2) * W_KBLK] for j in range(n)]
        return a, b

    def conv_in(cb):
        c0 = cb * CONV_BLK
        cs = slice(c0, c0 + CONV_BLK)
        xa, ca = pair_blocks(_dot(hn(), wconv_ref[:, 4 * c0:4 * c0 + 2 * CONV_BLK]))
        u_ref[SUBLANES:SUBLANES + TILE, cs] = jnp.concatenate([x * c for x, c in zip(xa, ca)], axis=1)
        u0 = u_ref[SUBLANES:SUBLANES + TILE, cs]
        u1 = u_ref[SUBLANES - 1:SUBLANES - 1 + TILE, cs]
        u2 = u_ref[SUBLANES - 2:SUBLANES - 2 + TILE, cs]
        conv = (u2 * convw_ref[0:1, cs] + u1 * convw_ref[1:2, cs] + u0 * convw_ref[2:3, cs]
                + convb_ref[:, cs])
        u_ref[0:SUBLANES, cs] = u_ref[TILE:TILE + SUBLANES, cs]
        return conv

    def conv_gate(cb, conv):
        c0 = cb * CONV_BLK
        ba, za = pair_blocks(_dot(hn(), wconv_ref[:, 4 * c0 + 2 * CONV_BLK:4 * c0 + 4 * CONV_BLK]))
        gated = [b * _silu(z) for b, z in zip(ba, za)]
        return (jnp.concatenate(gated, axis=1) * conv).astype(BF16)

    def conv_out(ya_blks):
        ya_ref[...] = _dot(jnp.concatenate(ya_blks, axis=1), waout_ref[...])

    def causal_mask():
        return (lax.broadcasted_iota(jnp.int32, (TILE, TILE), 0)
                >= lax.broadcasted_iota(jnp.int32, (TILE, TILE), 1))

    def gate_proj():
        gates = _dot(hn(), wgates_ref[...]) + bgates_ref[...]
        lf = jnp.minimum(gates, 0.0) - jnp.log1p(jnp.exp(-jnp.abs(gates)))
        lf_hi = lf.astype(BF16)
        lf_lo = (lf - lf_hi.astype(F32)).astype(BF16)
        return gates, lf_hi, lf_lo

    def gate_cumsum(gates, lf_hi, lf_lo):
        tri = causal_mask().astype(BF16)
        bcum = _dot(jnp.concatenate([tri, tri], axis=1), jnp.concatenate([lf_hi, lf_lo], axis=0))
        lane = lax.broadcasted_iota(jnp.int32, (TILE, LANES), 1)
        gcol = jnp.where(lane < N_HEADS, gates, bcum)
        return gcol, gcol.T

    def head_decay(h, gcol, grow):
        li_row = grow[h:h + 1, :]
        b_row = grow[N_HEADS + h:N_HEADS + h + 1, :]
        b_col = gcol[:, N_HEADS + h:N_HEADS + h + 1]
        m_st = m_ref[h, 0:1, 0:1]
        dmat = jnp.where(causal_mask(), b_col - b_row + li_row, -jnp.inf)
        a = b_col + m_st
        m_row = jnp.maximum(a, jnp.max(dmat, axis=1, keepdims=True))
        b_last = b_row[:, TILE - 1:TILE]
        g_row = b_last - b_row + li_row
        m_new = jnp.maximum(b_last + m_st, jnp.max(g_row, axis=1, keepdims=True))
        m_ref[h] = jnp.broadcast_to(m_new, (SUBLANES, LANES))
        return dict(
            dexp=jnp.exp(dmat - m_row),
            inter=jnp.exp(a - m_row),
            floor=jnp.exp(-m_row),
            w_row=jnp.exp(g_row - m_new),
            decay=jnp.exp(b_last + m_st - m_new),
        )

    def head_proj(h):
        q = _dot(hn(), wq_ref[:, h * DK:(h + 1) * DK])
        kt = lax.dot_general(wkt_ref[h * DK:(h + 1) * DK, :], hn(),
                             (((1,), (1,)), ((), ())), preferred_element_type=F32)
        v = _dot(hn(), wv_ref[:, h * DV:(h + 1) * DV]).astype(BF16)
        return q, kt, v

    def head_scores(q, kt):
        return _dot(q.astype(BF16), kt.astype(BF16))

    def head_num(h, q, v, s, dk):
        p32 = s * dk["dexp"]
        qs = (q * dk["inter"]).astype(BF16)
        num = _dot(jnp.concatenate([p32.astype(BF16), qs], axis=1),
                   jnp.concatenate([v, c_ref[h].astype(BF16)], axis=0))
        n_row = n_ref[h, 0:1, :]
        den = (jnp.sum(p32, axis=1, keepdims=True)
               + dk["inter"] * jnp.sum(q * n_row, axis=1, keepdims=True))
        return num / jnp.maximum(jnp.abs(den), dk["floor"])

    def head_ktw(h, kt, dk):
        ktw = kt * dk["w_row"]
        n_col = jnp.sum(ktw, axis=1, keepdims=True)
        n_add = jnp.broadcast_to(n_col, (DK, LANES)).T[0:SUBLANES, :]
        n_ref[h] = dk["decay"] * n_ref[h] + n_add
        return ktw.astype(BF16)

    def head_upd(h, ktw, v, dk):
        c_ref[h] = dk["decay"] * c_ref[h] + _dot(ktw, v)

    def head_gate(h):
        og, zb = pair_blocks(_dot(hn(), wogz_ref[:, 2 * h * DV:2 * (h + 1) * DV]))
        return jnp.concatenate([_sigmoid(o) * _silu(z) for o, z in zip(og, zb)], axis=1)

    def head_y(h, hh, gate):
        return (_rms(hh, ghead_ref[:, h * DV:(h + 1) * DV]) * gate).astype(BF16)

    def heads_out(yb_blks):
        yb_ref[...] = _dot(jnp.concatenate(yb_blks, axis=1), wbout_ref[...])

    def tail_gates():
        return pair_blocks(_dot(hnp_ref[...], wgab_ref[...]))

    def tail_merge(ga, gb):
        merged = [_sigmoid(a) * ya_ref[:, j * W_KBLK:(j + 1) * W_KBLK]
                  + _sigmoid(b) * yb_ref[:, j * W_KBLK:(j + 1) * W_KBLK]
                  for j, (a, b) in enumerate(zip(ga, gb))]
        return jnp.concatenate(merged, axis=1).astype(BF16)

    def tail_out_proj(merged):
        x1 = xp_ref[...] + _dot(merged, wo_ref[...])
        return x1, _rms(x1, gple_ref[...]).astype(BF16)

    def tail_finish(x1, r1):
        pe = _dot(pp_ref[...].astype(BF16), wple_ref[...])
        ple_gate = _sigmoid(_dot(r1, wpg_ref[...]))
        out_ref[...] = _rms(x1 + ple_gate * pe, gfinal_ref[...])

    @pl.when(step == pl.num_programs(0) - 1)
    def _():
        tail_finish(*tail_out_proj(tail_merge(*tail_gates())))

    @pl.when(step < pl.num_programs(0) - 1)
    def _():
        ga, gb = tail_gates()
        hn_ref[...] = _rms(x_ref[...], gmix_ref[...]).astype(BF16)
        gparts = gate_proj()
        conv0 = conv_in(0)
        x1, r1 = tail_out_proj(tail_merge(ga, gb))
        gcol, grow = gate_cumsum(*gparts)
        ya0 = conv_gate(0, conv0)
        dk0 = head_decay(0, gcol, grow)
        conv1 = conv_in(1)
        dk1 = head_decay(1, gcol, grow)
        tail_finish(x1, r1)
        ya1 = conv_gate(1, conv1)

        q0, kt0, v0 = head_proj(0)
        conv_out([ya0, ya1])
        s0 = head_scores(q0, kt0)
        q1, kt1, v1 = head_proj(1)
        hh0 = head_num(0, q0, v0, s0, dk0)
        ktw0 = head_ktw(0, kt0, dk0)
        s1 = head_scores(q1, kt1)
        dk2 = head_decay(2, gcol, grow)
        gate0 = head_gate(0)
        q2, kt2, v2 = head_proj(2)
        yb0 = head_y(0, hh0, gate0)
        hh1 = head_num(1, q1, v1, s1, dk1)
        ktw1 = head_ktw(1, kt1, dk1)
        s2 = head_scores(q2, kt2)
        dk3 = head_decay(3, gcol, grow)
        gate1 = head_gate(1)
        q3, kt3, v3 = head_proj(3)
        yb1 = head_y(1, hh1, gate1)
        hh2 = head_num(2, q2, v2, s2, dk2)
        ktw2 = head_ktw(2, kt2, dk2)
        s3 = head_scores(q3, kt3)
        gate2 = head_gate(2)
        yb2 = head_y(2, hh2, gate2)
        hh3 = head_num(3, q3, v3, s3, dk3)
        ktw3 = head_ktw(3, kt3, dk3)
        gate3 = head_gate(3)
        head_upd(0, ktw0, v0, dk0)
        head_upd(1, ktw1, v1, dk1)
        head_upd(2, ktw2, v2, dk2)
        heads_out([yb0, yb1, yb2, head_y(3, hh3, gate3)])
        head_upd(3, ktw3, v3, dk3)
        hnp_ref[...] = hn()


def _const_spec(shape):
    return pl.BlockSpec(shape, lambda s: (0,) * len(shape), pipeline_mode=pl.Buffered(1))


def kernel(x, p, g_mix, w_in, conv_w, conv_b, w_a_out, b_gates, g_head, w_b_out, w_o, g_ple,
           w_ple_gate, w_ple, g_final):
    bsz, seq, _ = x.shape
    assert w_in.shape[0] == 1 and seq % TILE == 0
    tiles_per_row = seq // TILE
    n_tiles = bsz * tiles_per_row
    assert w_in.shape[2] == 4 * CONV_WIDTH + 2 * QK_DIM + 3 * V_DIM + 2 * N_HEADS + 2 * D_MODEL
    wt = jnp.transpose(w_in[0])
    b_g = jnp.pad(b_gates[0], (0, LANES - 2 * N_HEADS)).reshape(1, LANES)

    hbm_weights = (wt, w_a_out[0], w_b_out[0], w_o[0], w_ple_gate[0], w_ple[0])
    consts = (
        g_mix[0].reshape(1, D_MODEL), b_g, conv_w[0], conv_b[0].reshape(1, CONV_WIDTH),
        g_head[0].reshape(1, V_DIM), g_ple[0].reshape(1, D_MODEL), g_final.reshape(1, D_MODEL),
    )

    def cur_map(s):
        c = jnp.minimum(s, n_tiles - 1)
        return (c // tiles_per_row, c % tiles_per_row, 0)

    def prev_map(s):
        c = jnp.maximum(s - 1, 0)
        return (c // tiles_per_row, c % tiles_per_row, 0)

    in_specs = [
        pl.BlockSpec((None, TILE, D_MODEL), cur_map),
        pl.BlockSpec((None, TILE, D_MODEL), prev_map),
        pl.BlockSpec((None, TILE, PLE_DIM), prev_map),
    ] + [pl.BlockSpec(memory_space=pl.ANY)] * len(hbm_weights) + [_const_spec(a.shape) for a in consts]

    return pl.pallas_call(
        functools.partial(_block_kernel, tiles_per_row),
        grid=(n_tiles + 1,),
        in_specs=in_specs,
        out_specs=pl.BlockSpec((None, TILE, D_MODEL), prev_map),
        out_shape=jax.ShapeDtypeStruct(x.shape, x.dtype),
        scratch_shapes=[
            pltpu.VMEM((N_HEADS, DK, DV), F32),
            pltpu.VMEM((N_HEADS, SUBLANES, DK), F32),
            pltpu.VMEM((N_HEADS, SUBLANES, LANES), F32),
            pltpu.VMEM((TILE + SUBLANES, CONV_WIDTH), F32),
            pltpu.VMEM((TILE, D_MODEL), BF16),
            pltpu.VMEM((TILE, D_MODEL), BF16),
            pltpu.VMEM((TILE, D_MODEL), F32),
            pltpu.VMEM((TILE, D_MODEL), F32),
            pltpu.VMEM((D_MODEL, 4 * CONV_WIDTH + QK_DIM + 3 * V_DIM + 2 * D_MODEL), BF16),
            pltpu.VMEM((QK_DIM, D_MODEL), BF16),
            pltpu.VMEM((D_MODEL, LANES), BF16),
            pltpu.VMEM((CONV_WIDTH, D_MODEL), BF16),
            pltpu.VMEM((V_DIM, D_MODEL), BF16),
            pltpu.VMEM((D_MODEL, D_MODEL), BF16),
            pltpu.VMEM((D_MODEL, D_MODEL), BF16),
            pltpu.VMEM((PLE_DIM, D_MODEL), BF16),
            pltpu.VMEM((2, W_KBLK, D_MODEL), F32),
            pltpu.VMEM((2 * N_HEADS, D_MODEL), F32),
            pltpu.SemaphoreType.DMA((4,)),
        ],
        compiler_params=pltpu.CompilerParams(
            dimension_semantics=("arbitrary",),
            vmem_limit_bytes=VMEM_LIMIT_BYTES,
        ),
        name="hybrid_block",
    )(x, x, p[0], *hbm_weights, *consts)
```

```python
import functools

import jax
import jax.numpy as jnp
from jax import lax
from jax.experimental import pallas as pl
from jax.experimental.pallas import tpu as pltpu

D_MODEL = 1024
PLE_DIM = 256
CONV_WIDTH = 1024
CONV_K = 3
N_HEADS = 4
QK_DIM = 1024
V_DIM = 2048
DK = QK_DIM // N_HEADS
DV = V_DIM // N_HEADS
EPS = 1e-6

LANES = 128
SUBLANES = 8
TILE = 256
CONV_BLK = 512
W_KBLK = 256
OGZ_PIECE0 = (4 * CONV_WIDTH + QK_DIM + V_DIM) // W_KBLK
GAB_PIECE0 = OGZ_PIECE0 + 2 * V_DIM // W_KBLK
GATE_ROWS = 16
VMEM_LIMIT_BYTES = 61 * 1024 * 1024

BF16 = jnp.bfloat16
F32 = jnp.float32


def _dot(a, b):
    return jnp.dot(a, b, preferred_element_type=F32)


def _sigmoid(v):
    return 0.5 * jnp.tanh(0.5 * v) + 0.5


def _silu(v):
    return v * _sigmoid(v)


def _rms(v, g):
    return v * lax.rsqrt(jnp.mean(v * v, axis=-1, keepdims=True) + EPS) * g


def _load_weights(wt_hbm, waout_hbm, wbout_hbm, wo_hbm, wpg_hbm, wple_hbm,
                  win_ref, wkt_ref, waout_ref, wbout_ref, wo_ref, wpg_ref, wple_ref,
                  slots, gstage_ref, sem):
    depth = len(slots)
    eye = (lax.broadcasted_iota(jnp.int32, (W_KBLK, W_KBLK), 0)
           == lax.broadcasted_iota(jnp.int32, (W_KBLK, W_KBLK), 1)).astype(BF16)

    def copy(src_hbm, row, s):
        return pltpu.make_async_copy(src_hbm.at[pl.ds(row, W_KBLK), :], slots[s], sem.at[s])

    k_row0 = 4 * CONV_WIDTH + QK_DIM
    gate_row0 = k_row0 + QK_DIM + 3 * V_DIM
    n_pieces = win_ref.shape[1] // W_KBLK
    assert n_pieces % depth == 0

    def src_row(p):
        blk = CONV_WIDTH // W_KBLK
        c = (p // 8) * 2 + (p // 2) % 2
        conv = c + blk * ((p // 4) % 2) + 2 * blk * (p % 2)
        e = p - OGZ_PIECE0
        ogz = (4 * CONV_WIDTH + 2 * QK_DIM + V_DIM) // W_KBLK + 2 * (e // 4) + (e // 2) % 2 \
            + (V_DIM // W_KBLK) * (e % 2)
        m = p - GAB_PIECE0
        pm = jnp.where(p >= GAB_PIECE0, GAB_PIECE0 + m // 2 + (D_MODEL // W_KBLK) * (m % 2), p)
        plain = pm + jnp.where(pm >= k_row0 // W_KBLK, QK_DIM // W_KBLK, 0)
        piece = jnp.where(p < 4 * blk, conv, plain)
        piece = jnp.where(jnp.logical_and(p >= OGZ_PIECE0, p < OGZ_PIECE0 + 2 * V_DIM // W_KBLK),
                          ogz, piece)
        r = piece * W_KBLK
        return pl.multiple_of(r + jnp.where(r >= gate_row0, 2 * N_HEADS, 0), SUBLANES)

    for s in range(depth):
        copy(wt_hbm, src_row(s), s).start(priority=s % 2)

    def group(g, carry):
        for s in range(depth):
            p = g * depth + s
            copy(wt_hbm, src_row(p), s).wait()
            blk = slots[s][...]
            is_q = jnp.logical_and(p * W_KBLK >= 4 * CONV_WIDTH, p * W_KBLK < k_row0)
            blk = blk * jnp.where(is_q, DK ** -0.5, 1.0).astype(F32)
            blk_t = lax.dot_general(blk.astype(BF16), eye, (((0,), (0,)), ((), ())),
                                    preferred_element_type=F32)
            win_ref[:, pl.ds(pl.multiple_of(p * W_KBLK, W_KBLK), W_KBLK)] = blk_t.astype(BF16)

            @pl.when(p + depth < n_pieces)
            def _():
                copy(wt_hbm, src_row(p + depth), s).start(priority=s % 2)
        return carry

    lax.fori_loop(0, n_pieces // depth, group, 0)

    plain = [(wt_hbm, k_row0 + h * DK + r, wkt_ref, ((h - 1) % N_HEADS) * DK + r)
             for h in range(N_HEADS) for r in range(0, DK, W_KBLK)]
    for src_hbm, dst_ref in ((waout_hbm, waout_ref), (wbout_hbm, wbout_ref), (wo_hbm, wo_ref),
                             (wpg_hbm, wpg_ref), (wple_hbm, wple_ref)):
        assert dst_ref.shape[1] == D_MODEL and dst_ref.shape[0] % W_KBLK == 0
        plain += [(src_hbm, r, dst_ref, r) for r in range(0, dst_ref.shape[0], W_KBLK)]
    for i in range(min(depth, len(plain))):
        copy(plain[i][0], plain[i][1], i % depth).start(priority=i % 2)
    for i, (src_hbm, row, dst_ref, dst_row) in enumerate(plain):
        copy(src_hbm, row, i % depth).wait()
        dst_ref[dst_row:dst_row + W_KBLK, :] = slots[i % depth][...].astype(BF16)
        if i + depth < len(plain):
            nxt = plain[i + depth]
            copy(nxt[0], nxt[1], i % depth).start(priority=i % 2)

    gcopy = pltpu.make_async_copy(wt_hbm.at[pl.ds(gate_row0, 2 * N_HEADS), :], gstage_ref, sem.at[0])
    gcopy.start()
    gcopy.wait()
    gpad = jnp.concatenate([gstage_ref[...], jnp.zeros((GATE_ROWS - 2 * N_HEADS, D_MODEL), F32)], axis=0)
    wkt_ref[QK_DIM:QK_DIM + GATE_ROWS, :] = gpad.astype(BF16)


def _block_kernel(tiles_per_row,
                  x_ref, xp_ref, pp_ref, wt_hbm, waout_hbm, wbout_hbm, wo_hbm, wpg_hbm, wple_hbm,
                  gmix_ref, bgates_ref, convw_ref, convb_ref, ghead_ref, gple_ref, gfinal_ref,
                  out_ref,
                  c_ref, n_ref, m_ref, u_ref, hn_ref, hnp_ref, ya_ref, yb_ref,
                  win_ref, wkt_ref, waout_ref, wbout_ref, wo_ref, wpg_ref, wple_ref,
                  stage_ref, gstage_ref, dma_sem):
    step = pl.program_id(0)

    o = 0
    wconv_ref = win_ref.at[:, o:o + 4 * CONV_WIDTH]; o += 4 * CONV_WIDTH
    wq_ref = win_ref.at[:, o:o + QK_DIM]; o += QK_DIM
    wv_ref = win_ref.at[:, o:o + V_DIM]; o += V_DIM
    wogz_ref = win_ref.at[:, o:o + 2 * V_DIM]; o += 2 * V_DIM
    wgab_ref = win_ref.at[:, o:o + 2 * D_MODEL]

    @pl.when(step == 0)
    def _():
        _load_weights(wt_hbm, waout_hbm, wbout_hbm, wo_hbm, wpg_hbm, wple_hbm,
                      win_ref, wkt_ref, waout_ref, wbout_ref, wo_ref, wpg_ref, wple_ref,
                      [stage_ref.at[0], stage_ref.at[1], ya_ref, yb_ref], gstage_ref, dma_sem)
        for ref in (ya_ref, yb_ref, hnp_ref):
            ref[...] = jnp.zeros_like(ref)

    @pl.when(lax.rem(step, tiles_per_row) == 0)
    def _():
        c_ref[...] = jnp.zeros_like(c_ref)
        n_ref[...] = jnp.zeros_like(n_ref)
        m_ref[...] = jnp.full_like(m_ref, -jnp.inf)
        u_ref[0:SUBLANES, :] = jnp.zeros((SUBLANES, CONV_WIDTH), F32)

    def hn():
        return hn_ref[...]

    def pair_blocks(hn_w):
        n = hn_w.shape[1] // (2 * W_KBLK)
        a = [hn_w[:, (2 * j) * W_KBLK:(2 * j + 1) * W_KBLK] for j in range(n)]
        b = [hn_w[:, (2 * j + 1) * W_KBLK:(2 * j + 2) * W_KBLK] for j in range(n)]
        return a, b

    def conv_in(cb):
        c0 = cb * CONV_BLK
        cs = slice(c0, c0 + CONV_BLK)
        xa, ca = pair_blocks(_dot(hn(), wconv_ref[:, 4 * c0:4 * c0 + 2 * CONV_BLK]))
        u_ref[SUBLANES:SUBLANES + TILE, cs] = jnp.concatenate([x * c for x, c in zip(xa, ca)], axis=1)
        u0 = u_ref[SUBLANES:SUBLANES + TILE, cs]
        u1 = u_ref[SUBLANES - 1:SUBLANES - 1 + TILE, cs]
        u2 = u_ref[SUBLANES - 2:SUBLANES - 2 + TILE, cs]
        conv = (u2 * convw_ref[0:1, cs] + u1 * convw_ref[1:2, cs] + u0 * convw_ref[2:3, cs]
                + convb_ref[:, cs])
        u_ref[0:SUBLANES, cs] = u_ref[TILE:TILE + SUBLANES, cs]
        return conv

    def conv_gate(cb, conv):
        c0 = cb * CONV_BLK
        ba, za = pair_blocks(_dot(hn(), wconv_ref[:, 4 * c0 + 2 * CONV_BLK:4 * c0 + 4 * CONV_BLK]))
        gated = [b * _silu(z) for b, z in zip(ba, za)]
        return (jnp.concatenate(gated, axis=1) * conv).astype(BF16)

    def conv_out(ya_blks):
        ya_ref[...] = _dot(jnp.concatenate(ya_blks, axis=1), waout_ref[...])

    def causal_mask():
        return (lax.broadcasted_iota(jnp.int32, (TILE, TILE), 0)
                >= lax.broadcasted_iota(jnp.int32, (TILE, TILE), 1))

    def proj_kt(h):
        r0 = ((h - 1) % N_HEADS) * DK
        rows = DK + GATE_ROWS if h == 0 else DK
        res = lax.dot_general(wkt_ref[r0:r0 + rows, :], hn(), (((1,), (1,)), ((), ())),
                              preferred_element_type=F32)
        return (res[0:DK], res[DK:DK + 2 * N_HEADS]) if h == 0 else res

    def gate_logs(graw):
        gates = graw + bgates_ref[:, 0:1]
        lf = jnp.minimum(gates, 0.0) - jnp.log1p(jnp.exp(-jnp.abs(gates)))
        lf_hi = lf.astype(BF16)
        lf_lo = (lf - lf_hi.astype(F32)).astype(BF16)
        triu = (lax.broadcasted_iota(jnp.int32, (TILE, TILE), 0)
                <= lax.broadcasted_iota(jnp.int32, (TILE, TILE), 1)).astype(BF16)
        bcum = _dot(jnp.concatenate([lf_hi, lf_lo], axis=1), jnp.concatenate([triu, triu], axis=0))
        row = lax.broadcasted_iota(jnp.int32, (2 * N_HEADS, TILE), 0)
        grow = jnp.where(row < N_HEADS, gates, bcum)
        gcol = jnp.concatenate([grow, jnp.zeros((LANES - 2 * N_HEADS, TILE), F32)], axis=0).T
        return gcol, grow

    def head_decay(h, gcol, grow):
        li_row = grow[h:h + 1, :]
        b_row = grow[N_HEADS + h:N_HEADS + h + 1, :]
        b_col = gcol[:, N_HEADS + h:N_HEADS + h + 1]
        m_st = m_ref[h, 0:1, 0:1]
        dmat = jnp.where(causal_mask(), b_col - b_row + li_row, -jnp.inf)
        a = b_col + m_st
        m_row = jnp.maximum(a, jnp.max(dmat, axis=1, keepdims=True))
        b_last = b_row[:, TILE - 1:TILE]
        g_row = b_last - b_row + li_row
        m_new = jnp.maximum(b_last + m_st, jnp.max(g_row, axis=1, keepdims=True))
        m_ref[h] = jnp.broadcast_to(m_new, (SUBLANES, LANES))
        return dict(
            dexp=jnp.exp(dmat - m_row),
            inter=jnp.exp(a - m_row),
            floor=jnp.exp(-m_row),
            w_row=jnp.exp(g_row - m_new),
            decay=jnp.exp(b_last + m_st - m_new),
        )

    def head_proj(h, kt=None):
        q = _dot(hn(), wq_ref[:, h * DK:(h + 1) * DK])
        if kt is None:
            kt = proj_kt(h)
        v = _dot(hn(), wv_ref[:, h * DV:(h + 1) * DV]).astype(BF16)
        return q, kt, v

    def head_scores(q, kt):
        return _dot(q.astype(BF16), kt.astype(BF16))

    def head_num(h, q, v, s, dk):
        p32 = s * dk["dexp"]
        qs = (q * dk["inter"]).astype(BF16)
        num = _dot(jnp.concatenate([p32.astype(BF16), qs], axis=1),
                   jnp.concatenate([v, c_ref[h].astype(BF16)], axis=0))
        n_row = n_ref[h, 0:1, :]
        den = (jnp.sum(p32, axis=1, keepdims=True)
               + dk["inter"] * jnp.sum(q * n_row, axis=1, keepdims=True))
        return num / jnp.maximum(jnp.abs(den), dk["floor"])

    def head_ktw(h, kt, dk):
        ktw = kt * dk["w_row"]
        n_col = jnp.sum(ktw, axis=1, keepdims=True)
        n_add = jnp.broadcast_to(n_col, (DK, LANES)).T[0:SUBLANES, :]
        n_ref[h] = dk["decay"] * n_ref[h] + n_add
        return ktw.astype(BF16)

    def head_upd(h, ktw, v, dk):
        c_ref[h] = dk["decay"] * c_ref[h] + _dot(ktw, v)

    def head_gate(h):
        og, zb = pair_blocks(_dot(hn(), wogz_ref[:, 2 * h * DV:2 * (h + 1) * DV]))
        return jnp.concatenate([_sigmoid(o) * _silu(z) for o, z in zip(og, zb)], axis=1)

    def head_y(h, hh, gate):
        return (_rms(hh, ghead_ref[:, h * DV:(h + 1) * DV]) * gate).astype(BF16)

    def heads_out(yb_blks):
        yb_ref[...] = _dot(jnp.concatenate(yb_blks, axis=1), wbout_ref[...])

    def tail_gates():
        return pair_blocks(_dot(hnp_ref[...], wgab_ref[...]))

    def tail_merge(ga, gb):
        merged = [_sigmoid(a) * ya_ref[:, j * W_KBLK:(j + 1) * W_KBLK]
                  + _sigmoid(b) * yb_ref[:, j * W_KBLK:(j + 1) * W_KBLK]
                  for j, (a, b) in enumerate(zip(ga, gb))]
        return jnp.concatenate(merged, axis=1).astype(BF16)

    def tail_out_proj(merged):
        x1 = xp_ref[...] + _dot(merged, wo_ref[...])
        return x1, _rms(x1, gple_ref[...]).astype(BF16)

    def tail_finish(x1, r1):
        pe = _dot(pp_ref[...].astype(BF16), wple_ref[...])
        ple_gate = _sigmoid(_dot(r1, wpg_ref[...]))
        out_ref[...] = _rms(x1 + ple_gate * pe, gfinal_ref[...])

    @pl.when(step == pl.num_programs(0) - 1)
    def _():
        tail_finish(*tail_out_proj(tail_merge(*tail_gates())))

    @pl.when(step < pl.num_programs(0) - 1)
    def _():
        ga, gb = tail_gates()
        hn_ref[...] = _rms(x_ref[...], gmix_ref[...]).astype(BF16)
        kt0, graw = proj_kt(0)
        conv0 = conv_in(0)
        x1, r1 = tail_out_proj(tail_merge(ga, gb))
        gcol, grow = gate_logs(graw)
        ya0 = conv_gate(0, conv0)
        dk0 = head_decay(0, gcol, grow)
        conv1 = conv_in(1)
        dk1 = head_decay(1, gcol, grow)
        tail_finish(x1, r1)
        ya1 = conv_gate(1, conv1)

        q0, kt0, v0 = head_proj(0, kt0)
        conv_out([ya0, ya1])
        s0 = head_scores(q0, kt0)
        q1, kt1, v1 = head_proj(1)
        hh0 = head_num(0, q0, v0, s0, dk0)
        ktw0 = head_ktw(0, kt0, dk0)
        s1 = head_scores(q1, kt1)
        dk2 = head_decay(2, gcol, grow)
        gate0 = head_gate(0)
        q2, kt2, v2 = head_proj(2)
        yb0 = head_y(0, hh0, gate0)
        hh1 = head_num(1, q1, v1, s1, dk1)
        ktw1 = head_ktw(1, kt1, dk1)
        s2 = head_scores(q2, kt2)
        dk3 = head_decay(3, gcol, grow)
        gate1 = head_gate(1)
        q3, kt3, v3 = head_proj(3)
        yb1 = head_y(1, hh1, gate1)
        hh2 = head_num(2, q2, v2, s2, dk2)
        ktw2 = head_ktw(2, kt2, dk2)
        s3 = head_scores(q3, kt3)
        gate2 = head_gate(2)
        yb2 = head_y(2, hh2, gate2)
        hh3 = head_num(3, q3, v3, s3, dk3)
        ktw3 = head_ktw(3, kt3, dk3)
        gate3 = head_gate(3)
        head_upd(0, ktw0, v0, dk0)
        head_upd(1, ktw1, v1, dk1)
        head_upd(2, ktw2, v2, dk2)
        heads_out([yb0, yb1, yb2, head_y(3, hh3, gate3)])
        head_upd(3, ktw3, v3, dk3)
        hnp_ref[...] = hn()


def _const_spec(shape):
    return pl.BlockSpec(shape, lambda s: (0,) * len(shape), pipeline_mode=pl.Buffered(1))


def kernel(x, p, g_mix, w_in, conv_w, conv_b, w_a_out, b_gates, g_head, w_b_out, w_o, g_ple,
           w_ple_gate, w_ple, g_final):
    bsz, seq, _ = x.shape
    assert w_in.shape[0] == 1 and seq % TILE == 0
    tiles_per_row = seq // TILE
    n_tiles = bsz * tiles_per_row
    assert w_in.shape[2] == 4 * CONV_WIDTH + 2 * QK_DIM + 3 * V_DIM + 2 * N_HEADS + 2 * D_MODEL
    wt = jnp.transpose(w_in[0])
    b_g = jnp.broadcast_to(b_gates[0].reshape(2 * N_HEADS, 1), (2 * N_HEADS, LANES))

    hbm_weights = (wt, w_a_out[0], w_b_out[0], w_o[0], w_ple_gate[0], w_ple[0])
    consts = (
        g_mix[0].reshape(1, D_MODEL), b_g, conv_w[0], conv_b[0].reshape(1, CONV_WIDTH),
        g_head[0].reshape(1, V_DIM), g_ple[0].reshape(1, D_MODEL), g_final.reshape(1, D_MODEL),
    )

    def cur_map(s):
        c = jnp.minimum(s, n_tiles - 1)
        return (c // tiles_per_row, c % tiles_per_row, 0)

    def prev_map(s):
        c = jnp.maximum(s - 1, 0)
        return (c // tiles_per_row, c % tiles_per_row, 0)

    in_specs = [
        pl.BlockSpec((None, TILE, D_MODEL), cur_map),
        pl.BlockSpec((None, TILE, D_MODEL), prev_map),
        pl.BlockSpec((None, TILE, PLE_DIM), prev_map),
    ] + [pl.BlockSpec(memory_space=pl.ANY)] * len(hbm_weights) + [_const_spec(a.shape) for a in consts]

    return pl.pallas_call(
        functools.partial(_block_kernel, tiles_per_row),
        grid=(n_tiles + 1,),
        in_specs=in_specs,
        out_specs=pl.BlockSpec((None, TILE, D_MODEL), prev_map),
        out_shape=jax.ShapeDtypeStruct(x.shape, x.dtype),
        scratch_shapes=[
            pltpu.VMEM((N_HEADS, DK, DV), F32),
            pltpu.VMEM((N_HEADS, SUBLANES, DK), F32),
            pltpu.VMEM((N_HEADS, SUBLANES, LANES), F32),
            pltpu.VMEM((TILE + SUBLANES, CONV_WIDTH), F32),
            pltpu.VMEM((TILE, D_MODEL), BF16),
            pltpu.VMEM((TILE, D_MODEL), BF16),
            pltpu.VMEM((TILE, D_MODEL), F32),
            pltpu.VMEM((TILE, D_MODEL), F32),
            pltpu.VMEM((D_MODEL, 4 * CONV_WIDTH + QK_DIM + 3 * V_DIM + 2 * D_MODEL), BF16),
            pltpu.VMEM((QK_DIM + GATE_ROWS, D_MODEL), BF16),
            pltpu.VMEM((CONV_WIDTH, D_MODEL), BF16),
            pltpu.VMEM((V_DIM, D_MODEL), BF16),
            pltpu.VMEM((D_MODEL, D_MODEL), BF16),
            pltpu.VMEM((D_MODEL, D_MODEL), BF16),
            pltpu.VMEM((PLE_DIM, D_MODEL), BF16),
            pltpu.VMEM((2, W_KBLK, D_MODEL), F32),
            pltpu.VMEM((2 * N_HEADS, D_MODEL), F32),
            pltpu.SemaphoreType.DMA((4,)),
        ],
        compiler_params=pltpu.CompilerParams(
            dimension_semantics=("arbitrary",),
            vmem_limit_bytes=VMEM_LIMIT_BYTES,
        ),
        name="hybrid_block",
    )(x, x, p[0], *hbm_weights, *consts)
```

```python
import functools

import jax
import jax.numpy as jnp
from jax import lax
from jax.experimental import pallas as pl
from jax.experimental.pallas import tpu as pltpu

D_MODEL = 1024
PLE_DIM = 256
CONV_WIDTH = 1024
CONV_K = 3
N_HEADS = 4
QK_DIM = 1024
V_DIM = 2048
DK = QK_DIM // N_HEADS
DV = V_DIM // N_HEADS
EPS = 1e-6

LANES = 128
SUBLANES = 8
TILE = 256
CONV_BLK = 512
W_KBLK = 256
OGZ_PIECE0 = (4 * CONV_WIDTH + QK_DIM + V_DIM) // W_KBLK
GAB_PIECE0 = OGZ_PIECE0 + 2 * V_DIM // W_KBLK
GATE_ROWS = 16
VMEM_LIMIT_BYTES = 61 * 1024 * 1024

BF16 = jnp.bfloat16
F32 = jnp.float32


def _dot(a, b):
    return jnp.dot(a, b, preferred_element_type=F32)


def _sigmoid(v):
    return 0.5 * jnp.tanh(0.5 * v) + 0.5


def _silu(v):
    return v * _sigmoid(v)


def _rms(v, g):
    return v * lax.rsqrt(jnp.mean(v * v, axis=-1, keepdims=True) + EPS) * g


def _load_weights(wt_hbm, waout_hbm, wbout_hbm, wo_hbm, wpg_hbm, wple_hbm,
                  win_ref, wkt_ref, waout_ref, wbout_ref, wo_ref, wpg_ref, wple_ref,
                  slots, gstage_ref, sem):
    depth = len(slots)
    eye = (lax.broadcasted_iota(jnp.int32, (W_KBLK, W_KBLK), 0)
           == lax.broadcasted_iota(jnp.int32, (W_KBLK, W_KBLK), 1)).astype(BF16)

    def copy(src_hbm, row, s):
        return pltpu.make_async_copy(src_hbm.at[pl.ds(row, W_KBLK), :], slots[s], sem.at[s])

    k_row0 = 4 * CONV_WIDTH + QK_DIM
    gate_row0 = k_row0 + QK_DIM + 3 * V_DIM
    n_pieces = win_ref.shape[1] // W_KBLK
    assert n_pieces % depth == 0

    def src_row(p):
        blk = CONV_WIDTH // W_KBLK
        c = (p // 8) * 2 + (p // 2) % 2
        conv = c + blk * ((p // 4) % 2) + 2 * blk * (p % 2)
        e = p - OGZ_PIECE0
        ogz = (4 * CONV_WIDTH + 2 * QK_DIM + V_DIM) // W_KBLK + 2 * (e // 4) + (e // 2) % 2 \
            + (V_DIM // W_KBLK) * (e % 2)
        m = p - GAB_PIECE0
        pm = jnp.where(p >= GAB_PIECE0, GAB_PIECE0 + m // 2 + (D_MODEL // W_KBLK) * (m % 2), p)
        plain = pm + jnp.where(pm >= k_row0 // W_KBLK, QK_DIM // W_KBLK, 0)
        piece = jnp.where(p < 4 * blk, conv, plain)
        piece = jnp.where(jnp.logical_and(p >= OGZ_PIECE0, p < OGZ_PIECE0 + 2 * V_DIM // W_KBLK),
                          ogz, piece)
        r = piece * W_KBLK
        return pl.multiple_of(r + jnp.where(r >= gate_row0, 2 * N_HEADS, 0), SUBLANES)

    for s in range(depth):
        copy(wt_hbm, src_row(s), s).start(priority=s % 2)

    def group(g, carry):
        for s in range(depth):
            p = g * depth + s
            copy(wt_hbm, src_row(p), s).wait()
            blk = slots[s][...]
            is_q = jnp.logical_and(p * W_KBLK >= 4 * CONV_WIDTH, p * W_KBLK < k_row0)
            blk = blk * jnp.where(is_q, DK ** -0.5, 1.0).astype(F32)
            blk_t = lax.dot_general(blk.astype(BF16), eye, (((0,), (0,)), ((), ())),
                                    preferred_element_type=F32)
            win_ref[:, pl.ds(pl.multiple_of(p * W_KBLK, W_KBLK), W_KBLK)] = blk_t.astype(BF16)

            @pl.when(p + depth < n_pieces)
            def _():
                copy(wt_hbm, src_row(p + depth), s).start(priority=s % 2)
        return carry

    lax.fori_loop(0, n_pieces // depth, group, 0)

    plain = [(wt_hbm, k_row0 + h * DK + r, wkt_ref, ((h - 1) % N_HEADS) * DK + r)
             for h in range(N_HEADS) for r in range(0, DK, W_KBLK)]
    for src_hbm, dst_ref in ((waout_hbm, waout_ref), (wbout_hbm, wbout_ref), (wo_hbm, wo_ref),
                             (wpg_hbm, wpg_ref), (wple_hbm, wple_ref)):
        assert dst_ref.shape[1] == D_MODEL and dst_ref.shape[0] % W_KBLK == 0
        plain += [(src_hbm, r, dst_ref, r) for r in range(0, dst_ref.shape[0], W_KBLK)]
    for i in range(min(depth, len(plain))):
        copy(plain[i][0], plain[i][1], i % depth).start(priority=i % 2)
    for i, (src_hbm, row, dst_ref, dst_row) in enumerate(plain):
        copy(src_hbm, row, i % depth).wait()
        dst_ref[dst_row:dst_row + W_KBLK, :] = slots[i % depth][...].astype(BF16)
        if i + depth < len(plain):
            nxt = plain[i + depth]
            copy(nxt[0], nxt[1], i % depth).start(priority=i % 2)

    gcopy = pltpu.make_async_copy(wt_hbm.at[pl.ds(gate_row0, 2 * N_HEADS), :], gstage_ref, sem.at[0])
    gcopy.start()
    gcopy.wait()
    gpad = jnp.concatenate([gstage_ref[...], jnp.zeros((GATE_ROWS - 2 * N_HEADS, D_MODEL), F32)], axis=0)
    wkt_ref[QK_DIM:QK_DIM + GATE_ROWS, :] = gpad.astype(BF16)


def _block_kernel(tiles_per_row,
                  x_ref, xp_ref, pp_ref, wt_hbm, waout_hbm, wbout_hbm, wo_hbm, wpg_hbm, wple_hbm,
                  gmix_ref, bgates_ref, convw_ref, convb_ref, ghead_ref, gple_ref, gfinal_ref,
                  out_ref,
                  c_ref, n_ref, m_ref, u_ref, hn_ref, hnp_ref, ya_ref, yb_ref,
                  win_ref, wkt_ref, waout_ref, wbout_ref, wo_ref, wpg_ref, wple_ref,
                  stage_ref, gstage_ref, dma_sem):
    step = pl.program_id(0)

    o = 0
    wconv_ref = win_ref.at[:, o:o + 4 * CONV_WIDTH]; o += 4 * CONV_WIDTH
    wq_ref = win_ref.at[:, o:o + QK_DIM]; o += QK_DIM
    wv_ref = win_ref.at[:, o:o + V_DIM]; o += V_DIM
    wogz_ref = win_ref.at[:, o:o + 2 * V_DIM]; o += 2 * V_DIM
    wgab_ref = win_ref.at[:, o:o + 2 * D_MODEL]

    @pl.when(step == 0)
    def _():
        _load_weights(wt_hbm, waout_hbm, wbout_hbm, wo_hbm, wpg_hbm, wple_hbm,
                      win_ref, wkt_ref, waout_ref, wbout_ref, wo_ref, wpg_ref, wple_ref,
                      [stage_ref.at[0], stage_ref.at[1], ya_ref, yb_ref], gstage_ref, dma_sem)
        for ref in (ya_ref, yb_ref, hnp_ref):
            ref[...] = jnp.zeros_like(ref)

    @pl.when(lax.rem(step, tiles_per_row) == 0)
    def _():
        c_ref[...] = jnp.zeros_like(c_ref)
        n_ref[...] = jnp.zeros_like(n_ref)
        m_ref[...] = jnp.full_like(m_ref, -jnp.inf)
        u_ref[0:SUBLANES, :] = jnp.zeros((SUBLANES, CONV_WIDTH), F32)

    def hn():
        return hn_ref[...]

    def pair_blocks(hn_w):
        n = hn_w.shape[1] // (2 * W_KBLK)
        a = [hn_w[:, (2 * j) * W_KBLK:(2 * j + 1) * W_KBLK] for j in range(n)]
        b = [hn_w[:, (2 * j + 1) * W_KBLK:(2 * j + 2) * W_KBLK] for j in range(n)]
        return a, b

    def conv_in(cb):
        c0 = cb * CONV_BLK
        cs = slice(c0, c0 + CONV_BLK)
        xa, ca = pair_blocks(_dot(hn(), wconv_ref[:, 4 * c0:4 * c0 + 2 * CONV_BLK]))
        u_ref[SUBLANES:SUBLANES + TILE, cs] = jnp.concatenate([x * c for x, c in zip(xa, ca)], axis=1)
        u0 = u_ref[SUBLANES:SUBLANES + TILE, cs]
        u1 = u_ref[SUBLANES - 1:SUBLANES - 1 + TILE, cs]
        u2 = u_ref[SUBLANES - 2:SUBLANES - 2 + TILE, cs]
        conv = (u2 * convw_ref[0:1, cs] + u1 * convw_ref[1:2, cs] + u0 * convw_ref[2:3, cs]
                + convb_ref[:, cs])
        u_ref[0:SUBLANES, cs] = u_ref[TILE:TILE + SUBLANES, cs]
        return conv

    def conv_gate(cb, conv):
        c0 = cb * CONV_BLK
        ba, za = pair_blocks(_dot(hn(), wconv_ref[:, 4 * c0 + 2 * CONV_BLK:4 * c0 + 4 * CONV_BLK]))
        gated = [b * _silu(z) for b, z in zip(ba, za)]
        return (jnp.concatenate(gated, axis=1) * conv).astype(BF16)

    def conv_out(ya_blks):
        ya_ref[...] = _dot(jnp.concatenate(ya_blks, axis=1), waout_ref[...])

    def causal_mask():
        return (lax.broadcasted_iota(jnp.int32, (TILE, TILE), 0)
                >= lax.broadcasted_iota(jnp.int32, (TILE, TILE), 1))

    def proj_kt(h):
        r0 = ((h - 1) % N_HEADS) * DK
        rows = DK + GATE_ROWS if h == 0 else DK
        res = lax.dot_general(wkt_ref[r0:r0 + rows, :], hn(), (((1,), (1,)), ((), ())),
                              preferred_element_type=F32)
        return (res[0:DK], res[DK:DK + 2 * N_HEADS]) if h == 0 else res

    def gate_logs(graw):
        gates = graw + bgates_ref[:, 0:1]
        lf = jnp.minimum(gates, 0.0) - jnp.log1p(jnp.exp(-jnp.abs(gates)))
        lf_hi = lf.astype(BF16)
        lf_lo = (lf - lf_hi.astype(F32)).astype(BF16)
        triu = (lax.broadcasted_iota(jnp.int32, (TILE, TILE), 0)
                <= lax.broadcasted_iota(jnp.int32, (TILE, TILE), 1)).astype(BF16)
        bcum = _dot(jnp.concatenate([lf_hi, lf_lo], axis=1), jnp.concatenate([triu, triu], axis=0))
        return gates[0:N_HEADS], bcum[N_HEADS:2 * N_HEADS]

    def cummax_lanes(v):
        lane = lax.broadcasted_iota(jnp.int32, v.shape, 1)
        shift = 1
        while shift < v.shape[1]:
            v = jnp.maximum(v, jnp.where(lane >= shift, pltpu.roll(v, shift, axis=1), -jnp.inf))
            shift *= 2
        return v

    def chunk_terms(li, b):
        m_st = jnp.concatenate([m_ref[h, 0:1, 0:1] for h in range(N_HEADS)], axis=0)
        d = li - b
        m_row = b + jnp.maximum(m_st, cummax_lanes(d))
        inter = jnp.exp(b + m_st - m_row)
        floor = jnp.exp(-m_row)
        b_last = b[:, TILE - 1:TILE]
        g = b_last - b + li
        m_new = jnp.maximum(b_last + m_st, jnp.max(g, axis=1, keepdims=True))
        w = jnp.exp(g - m_new)
        decay = jnp.exp(b_last + m_st - m_new)
        for h in range(N_HEADS):
            m_ref[h] = jnp.broadcast_to(m_new[h:h + 1], (SUBLANES, LANES))
        rows = jnp.concatenate([b - m_row, inter, floor,
                                jnp.zeros((LANES - 3 * N_HEADS, TILE), F32)], axis=0)
        cols = rows.T
        return [dict(d_row=d[h:h + 1], c_col=cols[:, h:h + 1],
                     inter=cols[:, N_HEADS + h:N_HEADS + h + 1],
                     floor=cols[:, 2 * N_HEADS + h:2 * N_HEADS + h + 1],
                     w_row=w[h:h + 1], decay=decay[h:h + 1]) for h in range(N_HEADS)]

    def head_dexp(dk):
        return jnp.exp(jnp.where(causal_mask(), dk["c_col"] + dk["d_row"], -jnp.inf))

    def head_proj(h, kt=None):
        q = _dot(hn(), wq_ref[:, h * DK:(h + 1) * DK])
        if kt is None:
            kt = proj_kt(h)
        v = _dot(hn(), wv_ref[:, h * DV:(h + 1) * DV]).astype(BF16)
        return q, kt, v

    def head_scores(q, kt):
        return _dot(q.astype(BF16), kt.astype(BF16))

    def head_num(h, q, v, s, dk):
        p32 = s * head_dexp(dk)
        qs = (q * dk["inter"]).astype(BF16)
        num = _dot(jnp.concatenate([p32.astype(BF16), qs], axis=1),
                   jnp.concatenate([v, c_ref[h].astype(BF16)], axis=0))
        n_row = n_ref[h, 0:1, :]
        den = (jnp.sum(p32, axis=1, keepdims=True)
               + dk["inter"] * jnp.sum(q * n_row, axis=1, keepdims=True))
        return num / jnp.maximum(jnp.abs(den), dk["floor"])

    def head_ktw(h, kt, dk):
        ktw = kt * dk["w_row"]
        n_col = jnp.sum(ktw, axis=1, keepdims=True)
        n_add = jnp.broadcast_to(n_col, (DK, LANES)).T[0:SUBLANES, :]
        n_ref[h] = dk["decay"] * n_ref[h] + n_add
        return ktw.astype(BF16)

    def head_upd(h, ktw, v, dk):
        c_ref[h] = dk["decay"] * c_ref[h] + _dot(ktw, v)

    def head_gate(h):
        og, zb = pair_blocks(_dot(hn(), wogz_ref[:, 2 * h * DV:2 * (h + 1) * DV]))
        return jnp.concatenate([_sigmoid(o) * _silu(z) for o, z in zip(og, zb)], axis=1)

    def head_y(h, hh, gate):
        return (_rms(hh, ghead_ref[:, h * DV:(h + 1) * DV]) * gate).astype(BF16)

    def heads_out(yb_blks):
        yb_ref[...] = _dot(jnp.concatenate(yb_blks, axis=1), wbout_ref[...])

    def tail_gates():
        return pair_blocks(_dot(hnp_ref[...], wgab_ref[...]))

    def tail_merge(ga, gb):
        merged = [_sigmoid(a) * ya_ref[:, j * W_KBLK:(j + 1) * W_KBLK]
                  + _sigmoid(b) * yb_ref[:, j * W_KBLK:(j + 1) * W_KBLK]
                  for j, (a, b) in enumerate(zip(ga, gb))]
        return jnp.concatenate(merged, axis=1).astype(BF16)

    def tail_out_proj(merged):
        x1 = xp_ref[...] + _dot(merged, wo_ref[...])
        return x1, _rms(x1, gple_ref[...]).astype(BF16)

    def tail_finish(x1, r1):
        pe = _dot(pp_ref[...].astype(BF16), wple_ref[...])
        ple_gate = _sigmoid(_dot(r1, wpg_ref[...]))
        out_ref[...] = _rms(x1 + ple_gate * pe, gfinal_ref[...])

    @pl.when(step == pl.num_programs(0) - 1)
    def _():
        tail_finish(*tail_out_proj(tail_merge(*tail_gates())))

    @pl.when(step < pl.num_programs(0) - 1)
    def _():
        ga, gb = tail_gates()
        hn_ref[...] = _rms(x_ref[...], gmix_ref[...]).astype(BF16)
        kt0, graw = proj_kt(0)
        conv0 = conv_in(0)
        x1, r1 = tail_out_proj(tail_merge(ga, gb))
        dks = chunk_terms(*gate_logs(graw))
        ya0 = conv_gate(0, conv0)
        conv1 = conv_in(1)
        tail_finish(x1, r1)
        ya1 = conv_gate(1, conv1)

        q0, kt0, v0 = head_proj(0, kt0)
        conv_out([ya0, ya1])
        s0 = head_scores(q0, kt0)
        q1, kt1, v1 = head_proj(1)
        hh0 = head_num(0, q0, v0, s0, dks[0])
        ktw0 = head_ktw(0, kt0, dks[0])
        s1 = head_scores(q1, kt1)
        gate0 = head_gate(0)
        q2, kt2, v2 = head_proj(2)
        yb0 = head_y(0, hh0, gate0)
        hh1 = head_num(1, q1, v1, s1, dks[1])
        ktw1 = head_ktw(1, kt1, dks[1])
        s2 = head_scores(q2, kt2)
        gate1 = head_gate(1)
        q3, kt3, v3 = head_proj(3)
        yb1 = head_y(1, hh1, gate1)
        hh2 = head_num(2, q2, v2, s2, dks[2])
        ktw2 = head_ktw(2, kt2, dks[2])
        s3 = head_scores(q3, kt3)
        gate2 = head_gate(2)
        yb2 = head_y(2, hh2, gate2)
        hh3 = head_num(3, q3, v3, s3, dks[3])
        ktw3 = head_ktw(3, kt3, dks[3])
        gate3 = head_gate(3)
        head_upd(0, ktw0, v0, dks[0])
        head_upd(1, ktw1, v1, dks[1])
        head_upd(2, ktw2, v2, dks[2])
        heads_out([yb0, yb1, yb2, head_y(3, hh3, gate3)])
        head_upd(3, ktw3, v3, dks[3])
        hnp_ref[...] = hn()


def _const_spec(shape):
    return pl.BlockSpec(shape, lambda s: (0,) * len(shape), pipeline_mode=pl.Buffered(1))


def kernel(x, p, g_mix, w_in, conv_w, conv_b, w_a_out, b_gates, g_head, w_b_out, w_o, g_ple,
           w_ple_gate, w_ple, g_final):
    bsz, seq, _ = x.shape
    assert w_in.shape[0] == 1 and seq % TILE == 0
    tiles_per_row = seq // TILE
    n_tiles = bsz * tiles_per_row
    assert w_in.shape[2] == 4 * CONV_WIDTH + 2 * QK_DIM + 3 * V_DIM + 2 * N_HEADS + 2 * D_MODEL
    wt = jnp.transpose(w_in[0])
    b_g = jnp.broadcast_to(b_gates[0].reshape(2 * N_HEADS, 1), (2 * N_HEADS, LANES))

    hbm_weights = (wt, w_a_out[0], w_b_out[0], w_o[0], w_ple_gate[0], w_ple[0])
    consts = (
        g_mix[0].reshape(1, D_MODEL), b_g, conv_w[0], conv_b[0].reshape(1, CONV_WIDTH),
        g_head[0].reshape(1, V_DIM), g_ple[0].reshape(1, D_MODEL), g_final.reshape(1, D_MODEL),
    )

    def cur_map(s):
        c = jnp.minimum(s, n_tiles - 1)
        return (c // tiles_per_row, c % tiles_per_row, 0)

    def prev_map(s):
        c = jnp.maximum(s - 1, 0)
        return (c // tiles_per_row, c % tiles_per_row, 0)

    in_specs = [
        pl.BlockSpec((None, TILE, D_MODEL), cur_map),
        pl.BlockSpec((None, TILE, D_MODEL), prev_map),
        pl.BlockSpec((None, TILE, PLE_DIM), prev_map),
    ] + [pl.BlockSpec(memory_space=pl.ANY)] * len(hbm_weights) + [_const_spec(a.shape) for a in consts]

    return pl.pallas_call(
        functools.partial(_block_kernel, tiles_per_row),
        grid=(n_tiles + 1,),
        in_specs=in_specs,
        out_specs=pl.BlockSpec((None, TILE, D_MODEL), prev_map),
        out_shape=jax.ShapeDtypeStruct(x.shape, x.dtype),
        scratch_shapes=[
            pltpu.VMEM((N_HEADS, DK, DV), F32),
            pltpu.VMEM((N_HEADS, SUBLANES, DK), F32),
            pltpu.VMEM((N_HEADS, SUBLANES, LANES), F32),
            pltpu.VMEM((TILE + SUBLANES, CONV_WIDTH), F32),
            pltpu.VMEM((TILE, D_MODEL), BF16),
            pltpu.VMEM((TILE, D_MODEL), BF16),
            pltpu.VMEM((TILE, D_MODEL), F32),
            pltpu.VMEM((TILE, D_MODEL), F32),
            pltpu.VMEM((D_MODEL, 4 * CONV_WIDTH + QK_DIM + 3 * V_DIM + 2 * D_MODEL), BF16),
            pltpu.VMEM((QK_DIM + GATE_ROWS, D_MODEL), BF16),
            pltpu.VMEM((CONV_WIDTH, D_MODEL), BF16),
            pltpu.VMEM((V_DIM, D_MODEL), BF16),
            pltpu.VMEM((D_MODEL, D_MODEL), BF16),
            pltpu.VMEM((D_MODEL, D_MODEL), BF16),
            pltpu.VMEM((PLE_DIM, D_MODEL), BF16),
            pltpu.VMEM((2, W_KBLK, D_MODEL), F32),
            pltpu.VMEM((2 * N_HEADS, D_MODEL), F32),
            pltpu.SemaphoreType.DMA((4,)),
        ],
        compiler_params=pltpu.CompilerParams(
            dimension_semantics=("arbitrary",),
            vmem_limit_bytes=VMEM_LIMIT_BYTES,
        ),
        name="hybrid_block",
    )(x, x, p[0], *hbm_weights, *consts)
```

```python
import functools

import jax
import jax.numpy as jnp
from jax import lax
from jax.experimental import pallas as pl
from jax.experimental.pallas import tpu as pltpu

D_MODEL = 1024
PLE_DIM = 256
CONV_WIDTH = 1024
CONV_K = 3
N_HEADS = 4
QK_DIM = 1024
V_DIM = 2048
DK = QK_DIM // N_HEADS
DV = V_DIM // N_HEADS
EPS = 1e-6

LANES = 128
SUBLANES = 8
TILE = 256
CONV_BLK = 512
W_KBLK = 256
OGZ_PIECE0 = (4 * CONV_WIDTH + QK_DIM + V_DIM) // W_KBLK
GAB_PIECE0 = OGZ_PIECE0 + 2 * V_DIM // W_KBLK
GATE_ROWS = 16
V7X_VMEM_BYTES = 64 * 1024 * 1024
VMEM_LIMIT_BYTES = V7X_VMEM_BYTES - 3 * 1024 * 1024

BF16 = jnp.bfloat16
F32 = jnp.float32


def _dot(a, b):
    return jnp.dot(a, b, preferred_element_type=F32)


def _sigmoid(v):
    return 0.5 * jnp.tanh(0.5 * v) + 0.5


def _silu(v):
    return v * _sigmoid(v)


def _rms(v, g):
    return v * lax.rsqrt(jnp.mean(v * v, axis=-1, keepdims=True) + EPS) * g


def _load_weights(wt_hbm, waout_hbm, wbout_hbm, wo_hbm, wpg_hbm, wple_hbm,
                  win_ref, wkt_ref, waout_ref, wbout_ref, wo_ref, wpg_ref, wple_ref,
                  slots, gstage_ref, sem):
    depth = len(slots)
    eye = (lax.broadcasted_iota(jnp.int32, (W_KBLK, W_KBLK), 0)
           == lax.broadcasted_iota(jnp.int32, (W_KBLK, W_KBLK), 1)).astype(BF16)

    def copy(src_hbm, row, s):
        return pltpu.make_async_copy(src_hbm.at[pl.ds(row, W_KBLK), :], slots[s], sem.at[s])

    k_row0 = 4 * CONV_WIDTH + QK_DIM
    gate_row0 = k_row0 + QK_DIM + 3 * V_DIM
    n_pieces = win_ref.shape[1] // W_KBLK
    assert n_pieces % depth == 0

    def src_row(p):
        blk = CONV_WIDTH // W_KBLK
        c = (p // 8) * 2 + (p // 2) % 2
        conv = c + blk * ((p // 4) % 2) + 2 * blk * (p % 2)
        e = p - OGZ_PIECE0
        ogz = (4 * CONV_WIDTH + 2 * QK_DIM + V_DIM) // W_KBLK + 2 * (e // 4) + (e // 2) % 2 \
            + (V_DIM // W_KBLK) * (e % 2)
        m = p - GAB_PIECE0
        pm = jnp.where(p >= GAB_PIECE0, GAB_PIECE0 + m // 2 + (D_MODEL // W_KBLK) * (m % 2), p)
        plain = pm + jnp.where(pm >= k_row0 // W_KBLK, QK_DIM // W_KBLK, 0)
        piece = jnp.where(p < 4 * blk, conv, plain)
        piece = jnp.where(jnp.logical_and(p >= OGZ_PIECE0, p < OGZ_PIECE0 + 2 * V_DIM // W_KBLK),
                          ogz, piece)
        r = piece * W_KBLK
        return pl.multiple_of(r + jnp.where(r >= gate_row0, 2 * N_HEADS, 0), SUBLANES)

    for s in range(depth):
        copy(wt_hbm, src_row(s), s).start(priority=s % 2)

    def group(g, carry):
        for s in range(depth):
            p = g * depth + s
            copy(wt_hbm, src_row(p), s).wait()
            blk = slots[s][...]
            is_q = jnp.logical_and(p * W_KBLK >= 4 * CONV_WIDTH, p * W_KBLK < k_row0)
            blk = blk * jnp.where(is_q, DK ** -0.5, 1.0).astype(F32)
            blk_t = lax.dot_general(blk.astype(BF16), eye, (((0,), (0,)), ((), ())),
                                    preferred_element_type=F32)
            win_ref[:, pl.ds(pl.multiple_of(p * W_KBLK, W_KBLK), W_KBLK)] = blk_t.astype(BF16)

            @pl.when(p + depth < n_pieces)
            def _():
                copy(wt_hbm, src_row(p + depth), s).start(priority=s % 2)
        return carry

    lax.fori_loop(0, n_pieces // depth, group, 0)

    plain = [(wt_hbm, k_row0 + h * DK + r, wkt_ref, ((h - 1) % N_HEADS) * DK + r)
             for h in range(N_HEADS) for r in range(0, DK, W_KBLK)]
    for src_hbm, dst_ref in ((waout_hbm, waout_ref), (wbout_hbm, wbout_ref), (wo_hbm, wo_ref),
                             (wpg_hbm, wpg_ref), (wple_hbm, wple_ref)):
        assert dst_ref.shape[1] == D_MODEL and dst_ref.shape[0] % W_KBLK == 0
        plain += [(src_hbm, r, dst_ref, r) for r in range(0, dst_ref.shape[0], W_KBLK)]
    for i in range(min(depth, len(plain))):
        copy(plain[i][0], plain[i][1], i % depth).start(priority=i % 2)
    for i, (src_hbm, row, dst_ref, dst_row) in enumerate(plain):
        copy(src_hbm, row, i % depth).wait()
        dst_ref[dst_row:dst_row + W_KBLK, :] = slots[i % depth][...].astype(BF16)
        if i + depth < len(plain):
            nxt = plain[i + depth]
            copy(nxt[0], nxt[1], i % depth).start(priority=i % 2)

    gcopy = pltpu.make_async_copy(wt_hbm.at[pl.ds(gate_row0, 2 * N_HEADS), :], gstage_ref, sem.at[0])
    gcopy.start()
    gcopy.wait()
    gpad = jnp.concatenate([gstage_ref[...], jnp.zeros((GATE_ROWS - 2 * N_HEADS, D_MODEL), F32)], axis=0)
    wkt_ref[QK_DIM:QK_DIM + GATE_ROWS, :] = gpad.astype(BF16)


def _block_kernel(tiles_per_row,
                  x_ref, xp_ref, pp_ref, wt_hbm, waout_hbm, wbout_hbm, wo_hbm, wpg_hbm, wple_hbm,
                  gmix_ref, bgates_ref, convw_ref, convb_ref, ghead_ref, gple_ref, gfinal_ref,
                  out_ref,
                  c_ref, n_ref, m_ref, u_ref, hn_ref, hnp_ref, ya_ref, yb_ref,
                  win_ref, wkt_ref, waout_ref, wbout_ref, wo_ref, wpg_ref, wple_ref,
                  stage_ref, gstage_ref, dma_sem):
    step = pl.program_id(0)

    o = 0
    wconv_ref = win_ref.at[:, o:o + 4 * CONV_WIDTH]; o += 4 * CONV_WIDTH
    wq_ref = win_ref.at[:, o:o + QK_DIM]; o += QK_DIM
    wv_ref = win_ref.at[:, o:o + V_DIM]; o += V_DIM
    wogz_ref = win_ref.at[:, o:o + 2 * V_DIM]; o += 2 * V_DIM
    wgab_ref = win_ref.at[:, o:o + 2 * D_MODEL]

    @pl.when(step == 0)
    def _():
        _load_weights(wt_hbm, waout_hbm, wbout_hbm, wo_hbm, wpg_hbm, wple_hbm,
                      win_ref, wkt_ref, waout_ref, wbout_ref, wo_ref, wpg_ref, wple_ref,
                      [stage_ref.at[0], stage_ref.at[1], ya_ref, yb_ref], gstage_ref, dma_sem)
        for ref in (ya_ref, yb_ref, hnp_ref):
            ref[...] = jnp.zeros_like(ref)

    @pl.when(lax.rem(step, tiles_per_row) == 0)
    def _():
        c_ref[...] = jnp.zeros_like(c_ref)
        n_ref[...] = jnp.zeros_like(n_ref)
        m_ref[...] = jnp.full_like(m_ref, -jnp.inf)
        u_ref[0:SUBLANES, :] = jnp.zeros((SUBLANES, CONV_WIDTH), F32)

    def hn():
        return hn_ref[...]

    def pair_blocks(hn_w):
        n = hn_w.shape[1] // (2 * W_KBLK)
        a = [hn_w[:, (2 * j) * W_KBLK:(2 * j + 1) * W_KBLK] for j in range(n)]
        b = [hn_w[:, (2 * j + 1) * W_KBLK:(2 * j + 2) * W_KBLK] for j in range(n)]
        return a, b

    def conv_in(cb):
        c0 = cb * CONV_BLK
        cs = slice(c0, c0 + CONV_BLK)
        xa, ca = pair_blocks(_dot(hn(), wconv_ref[:, 4 * c0:4 * c0 + 2 * CONV_BLK]))
        u_ref[SUBLANES:SUBLANES + TILE, cs] = jnp.concatenate([x * c for x, c in zip(xa, ca)], axis=1)
        u0 = u_ref[SUBLANES:SUBLANES + TILE, cs]
        u1 = u_ref[SUBLANES - 1:SUBLANES - 1 + TILE, cs]
        u2 = u_ref[SUBLANES - 2:SUBLANES - 2 + TILE, cs]
        conv = (u2 * convw_ref[0:1, cs] + u1 * convw_ref[1:2, cs] + u0 * convw_ref[2:3, cs]
                + convb_ref[:, cs])
        u_ref[0:SUBLANES, cs] = u_ref[TILE:TILE + SUBLANES, cs]
        return conv

    def conv_gate(cb, conv):
        c0 = cb * CONV_BLK
        ba, za = pair_blocks(_dot(hn(), wconv_ref[:, 4 * c0 + 2 * CONV_BLK:4 * c0 + 4 * CONV_BLK]))
        gated = [b * _silu(z) for b, z in zip(ba, za)]
        return (jnp.concatenate(gated, axis=1) * conv).astype(BF16)

    def conv_out(ya_blks):
        ya_ref[...] = _dot(jnp.concatenate(ya_blks, axis=1), waout_ref[...])

    def causal_mask():
        return (lax.broadcasted_iota(jnp.int32, (TILE, TILE), 0)
                >= lax.broadcasted_iota(jnp.int32, (TILE, TILE), 1))

    def proj_kt(h):
        r0 = ((h - 1) % N_HEADS) * DK
        rows = DK + GATE_ROWS if h == 0 else DK
        res = lax.dot_general(wkt_ref[r0:r0 + rows, :], hn(), (((1,), (1,)), ((), ())),
                              preferred_element_type=F32)
        return (res[0:DK], res[DK:DK + 2 * N_HEADS]) if h == 0 else res

    def gate_logs(graw):
        gates = graw + bgates_ref[:, 0:1]
        lf = jnp.minimum(gates, 0.0) - jnp.log1p(jnp.exp(-jnp.abs(gates)))
        lf_hi = lf.astype(BF16)
        lf_mid = (lf - lf_hi.astype(F32)).astype(BF16)
        lf_lo = (lf - lf_hi.astype(F32) - lf_mid.astype(F32)).astype(BF16)
        triu = (lax.broadcasted_iota(jnp.int32, (TILE, TILE), 0)
                <= lax.broadcasted_iota(jnp.int32, (TILE, TILE), 1)).astype(BF16)
        bcum = _dot(jnp.concatenate([lf_hi, lf_mid, lf_lo], axis=1),
                    jnp.concatenate([triu, triu, triu], axis=0))
        return gates[0:N_HEADS], bcum[N_HEADS:2 * N_HEADS]

    def cummax_lanes(v):
        lane = lax.broadcasted_iota(jnp.int32, v.shape, 1)
        shift = 1
        while shift < v.shape[1]:
            v = jnp.maximum(v, jnp.where(lane >= shift, pltpu.roll(v, shift, axis=1), -jnp.inf))
            shift *= 2
        return v

    def chunk_terms(li, b):
        m_st = jnp.concatenate([m_ref[h, 0:1, 0:1] for h in range(N_HEADS)], axis=0)
        d = li - b
        m_row = b + jnp.maximum(m_st, cummax_lanes(d))
        inter = jnp.exp(b + m_st - m_row)
        floor = jnp.exp(-m_row)
        b_last = b[:, TILE - 1:TILE]
        g = b_last - b + li
        m_new = jnp.maximum(b_last + m_st, jnp.max(g, axis=1, keepdims=True))
        w = jnp.exp(g - m_new)
        decay = jnp.exp(b_last + m_st - m_new)
        for h in range(N_HEADS):
            m_ref[h] = jnp.broadcast_to(m_new[h:h + 1], (SUBLANES, LANES))
        rows = jnp.concatenate([b - m_row, inter, floor,
                                jnp.zeros((LANES - 3 * N_HEADS, TILE), F32)], axis=0)
        cols = rows.T
        return [dict(d_row=d[h:h + 1], c_col=cols[:, h:h + 1],
                     inter=cols[:, N_HEADS + h:N_HEADS + h + 1],
                     floor=cols[:, 2 * N_HEADS + h:2 * N_HEADS + h + 1],
                     w_row=w[h:h + 1], decay=decay[h:h + 1]) for h in range(N_HEADS)]

    def head_dexp(dk):
        return jnp.exp(jnp.where(causal_mask(), dk["c_col"] + dk["d_row"], -jnp.inf))

    def head_proj(h, kt=None):
        q = _dot(hn(), wq_ref[:, h * DK:(h + 1) * DK])
        if kt is None:
            kt = proj_kt(h)
        v = _dot(hn(), wv_ref[:, h * DV:(h + 1) * DV]).astype(BF16)
        return q, kt, v

    def head_scores(q, kt):
        return _dot(q.astype(BF16), kt.astype(BF16))

    def head_num(h, q, v, s, dk):
        p32 = s * head_dexp(dk)
        qs = (q * dk["inter"]).astype(BF16)
        num = _dot(jnp.concatenate([p32.astype(BF16), qs], axis=1),
                   jnp.concatenate([v, c_ref[h].astype(BF16)], axis=0))
        n_row = n_ref[h, 0:1, :]
        den = (jnp.sum(p32, axis=1, keepdims=True)
               + dk["inter"] * jnp.sum(q * n_row, axis=1, keepdims=True))
        return num / jnp.maximum(jnp.abs(den), dk["floor"])

    def head_ktw(h, kt, dk):
        ktw = kt * dk["w_row"]
        n_col = jnp.sum(ktw, axis=1, keepdims=True)
        n_add = jnp.broadcast_to(n_col, (DK, LANES)).T[0:SUBLANES, :]
        n_ref[h] = dk["decay"] * n_ref[h] + n_add
        return ktw.astype(BF16)

    def head_upd(h, ktw, v, dk):
        c_ref[h] = dk["decay"] * c_ref[h] + _dot(ktw, v)

    def head_gate(h):
        og, zb = pair_blocks(_dot(hn(), wogz_ref[:, 2 * h * DV:2 * (h + 1) * DV]))
        return jnp.concatenate([_sigmoid(o) * _silu(z) for o, z in zip(og, zb)], axis=1)

    def head_y(h, hh, gate):
        return (_rms(hh, ghead_ref[:, h * DV:(h + 1) * DV]) * gate).astype(BF16)

    def heads_out(yb_blks):
        yb_ref[...] = _dot(jnp.concatenate(yb_blks, axis=1), wbout_ref[...])

    def tail_gates():
        return pair_blocks(_dot(hnp_ref[...], wgab_ref[...]))

    def tail_merge(ga, gb):
        merged = [_sigmoid(a) * ya_ref[:, j * W_KBLK:(j + 1) * W_KBLK]
                  + _sigmoid(b) * yb_ref[:, j * W_KBLK:(j + 1) * W_KBLK]
                  for j, (a, b) in enumerate(zip(ga, gb))]
        return jnp.concatenate(merged, axis=1).astype(BF16)

    def tail_out_proj(merged):
        x1 = xp_ref[...] + _dot(merged, wo_ref[...])
        return x1, _rms(x1, gple_ref[...]).astype(BF16)

    def tail_finish(x1, r1):
        pe = _dot(pp_ref[...].astype(BF16), wple_ref[...])
        ple_gate = _sigmoid(_dot(r1, wpg_ref[...]))
        out_ref[...] = _rms(x1 + ple_gate * pe, gfinal_ref[...])

    @pl.when(step == pl.num_programs(0) - 1)
    def _():
        tail_finish(*tail_out_proj(tail_merge(*tail_gates())))

    @pl.when(step < pl.num_programs(0) - 1)
    def _():
        ga, gb = tail_gates()
        hn_ref[...] = _rms(x_ref[...], gmix_ref[...]).astype(BF16)
        kt0, graw = proj_kt(0)
        conv0 = conv_in(0)
        x1, r1 = tail_out_proj(tail_merge(ga, gb))
        dks = chunk_terms(*gate_logs(graw))
        ya0 = conv_gate(0, conv0)
        conv1 = conv_in(1)
        tail_finish(x1, r1)
        ya1 = conv_gate(1, conv1)

        q0, kt0, v0 = head_proj(0, kt0)
        conv_out([ya0, ya1])
        s0 = head_scores(q0, kt0)
        q1, kt1, v1 = head_proj(1)
        hh0 = head_num(0, q0, v0, s0, dks[0])
        ktw0 = head_ktw(0, kt0, dks[0])
        s1 = head_scores(q1, kt1)
        gate0 = head_gate(0)
        q2, kt2, v2 = head_proj(2)
        yb0 = head_y(0, hh0, gate0)
        hh1 = head_num(1, q1, v1, s1, dks[1])
        ktw1 = head_ktw(1, kt1, dks[1])
        s2 = head_scores(q2, kt2)
        gate1 = head_gate(1)
        q3, kt3, v3 = head_proj(3)
        yb1 = head_y(1, hh1, gate1)
        hh2 = head_num(2, q2, v2, s2, dks[2])
        ktw2 = head_ktw(2, kt2, dks[2])
        s3 = head_scores(q3, kt3)
        gate2 = head_gate(2)
        yb2 = head_y(2, hh2, gate2)
        hh3 = head_num(3, q3, v3, s3, dks[3])
        ktw3 = head_ktw(3, kt3, dks[3])
        gate3 = head_gate(3)
        head_upd(0, ktw0, v0, dks[0])
        head_upd(1, ktw1, v1, dks[1])
        head_upd(2, ktw2, v2, dks[2])
        heads_out([yb0, yb1, yb2, head_y(3, hh3, gate3)])
        head_upd(3, ktw3, v3, dks[3])
        hnp_ref[...] = hn()


def _const_spec(shape):
    return pl.BlockSpec(shape, lambda s: (0,) * len(shape), pipeline_mode=pl.Buffered(1))


def kernel(x, p, g_mix, w_in, conv_w, conv_b, w_a_out, b_gates, g_head, w_b_out, w_o, g_ple,
           w_ple_gate, w_ple, g_final):
    bsz, seq, _ = x.shape
    assert w_in.shape[0] == 1 and seq % TILE == 0 and conv_w.shape[1] == CONV_K == 3
    tiles_per_row = seq // TILE
    n_tiles = bsz * tiles_per_row
    assert w_in.shape[2] == 4 * CONV_WIDTH + 2 * QK_DIM + 3 * V_DIM + 2 * N_HEADS + 2 * D_MODEL
    wt = jnp.transpose(w_in[0])
    b_g = jnp.broadcast_to(b_gates[0].reshape(2 * N_HEADS, 1), (2 * N_HEADS, LANES))

    hbm_weights = (wt, w_a_out[0], w_b_out[0], w_o[0], w_ple_gate[0], w_ple[0])
    consts = (
        g_mix[0].reshape(1, D_MODEL), b_g, conv_w[0], conv_b[0].reshape(1, CONV_WIDTH),
        g_head[0].reshape(1, V_DIM), g_ple[0].reshape(1, D_MODEL), g_final.reshape(1, D_MODEL),
    )

    def cur_map(s):
        c = jnp.minimum(s, n_tiles - 1)
        return (c // tiles_per_row, c % tiles_per_row, 0)

    def prev_map(s):
        c = jnp.maximum(s - 1, 0)
        return (c // tiles_per_row, c % tiles_per_row, 0)

    in_specs = [
        pl.BlockSpec((None, TILE, D_MODEL), cur_map),
        pl.BlockSpec((None, TILE, D_MODEL), prev_map),
        pl.BlockSpec((None, TILE, PLE_DIM), prev_map),
    ] + [pl.BlockSpec(memory_space=pl.ANY)] * len(hbm_weights) + [_const_spec(a.shape) for a in consts]

    return pl.pallas_call(
        functools.partial(_block_kernel, tiles_per_row),
        grid=(n_tiles + 1,),
        in_specs=in_specs,
        out_specs=pl.BlockSpec((None, TILE, D_MODEL), prev_map),
        out_shape=jax.ShapeDtypeStruct(x.shape, x.dtype),
        scratch_shapes=[
            pltpu.VMEM((N_HEADS, DK, DV), F32),
            pltpu.VMEM((N_HEADS, SUBLANES, DK), F32),
            pltpu.VMEM((N_HEADS, SUBLANES, LANES), F32),
            pltpu.VMEM((TILE + SUBLANES, CONV_WIDTH), F32),
            pltpu.VMEM((TILE, D_MODEL), BF16),
            pltpu.VMEM((TILE, D_MODEL), BF16),
            pltpu.VMEM((TILE, D_MODEL), F32),
            pltpu.VMEM((TILE, D_MODEL), F32),
            pltpu.VMEM((D_MODEL, 4 * CONV_WIDTH + QK_DIM + 3 * V_DIM + 2 * D_MODEL), BF16),
            pltpu.VMEM((QK_DIM + GATE_ROWS, D_MODEL), BF16),
            pltpu.VMEM((CONV_WIDTH, D_MODEL), BF16),
            pltpu.VMEM((V_DIM, D_MODEL), BF16),
            pltpu.VMEM((D_MODEL, D_MODEL), BF16),
            pltpu.VMEM((D_MODEL, D_MODEL), BF16),
            pltpu.VMEM((PLE_DIM, D_MODEL), BF16),
            pltpu.VMEM((2, W_KBLK, D_MODEL), F32),
            pltpu.VMEM((2 * N_HEADS, D_MODEL), F32),
            pltpu.SemaphoreType.DMA((4,)),
        ],
        compiler_params=pltpu.CompilerParams(
            dimension_semantics=("arbitrary",),
            vmem_limit_bytes=VMEM_LIMIT_BYTES,
        ),
        name="hybrid_block",
    )(x, x, p[0], *hbm_weights, *consts)
```

```python
import functools

import jax
import jax.numpy as jnp
from jax import lax
from jax.experimental import pallas as pl
from jax.experimental.pallas import tpu as pltpu

D_MODEL = 1024
PLE_DIM = 256
CONV_WIDTH = 1024
CONV_K = 3
N_HEADS = 4
QK_DIM = 1024
V_DIM = 2048
DK = QK_DIM // N_HEADS
DV = V_DIM // N_HEADS
EPS = 1e-6

LANES = 128
SUBLANES = 8
TILE = 256
CONV_BLK = 512
W_KBLK = 256
OGZ_PIECE0 = (4 * CONV_WIDTH + QK_DIM + V_DIM) // W_KBLK
GAB_PIECE0 = OGZ_PIECE0 + 2 * V_DIM // W_KBLK
GATE_ROWS = 16
V7X_VMEM_BYTES = 64 * 1024 * 1024
VMEM_LIMIT_BYTES = V7X_VMEM_BYTES - 3 * 1024 * 1024

BF16 = jnp.bfloat16
F32 = jnp.float32


def _dot(a, b):
    return jnp.dot(a, b, preferred_element_type=F32)


def _sigmoid(v):
    return 0.5 * jnp.tanh(0.5 * v) + 0.5


def _silu(v):
    return v * _sigmoid(v)


def _rms(v, g):
    return v * lax.rsqrt(jnp.mean(v * v, axis=-1, keepdims=True) + EPS) * g


def _load_weights(wt_hbm, waout_hbm, wbout_hbm, wo_hbm, wpg_hbm, wple_hbm,
                  win_ref, wkt_ref, waout_ref, wbout_ref, wo_ref, wpg_ref, wple_ref,
                  slots, gstage_ref, sem):
    depth = len(slots)
    eye = (lax.broadcasted_iota(jnp.int32, (W_KBLK, W_KBLK), 0)
           == lax.broadcasted_iota(jnp.int32, (W_KBLK, W_KBLK), 1)).astype(BF16)

    def copy(src_hbm, row, s):
        return pltpu.make_async_copy(src_hbm.at[pl.ds(row, W_KBLK), :], slots[s], sem.at[s])

    k_row0 = 4 * CONV_WIDTH + QK_DIM
    gate_row0 = k_row0 + QK_DIM + 3 * V_DIM
    n_pieces = win_ref.shape[1] // W_KBLK
    assert n_pieces % depth == 0

    def src_row(p):
        blk = CONV_WIDTH // W_KBLK
        c = (p // 8) * 2 + (p // 2) % 2
        conv = c + blk * ((p // 4) % 2) + 2 * blk * (p % 2)
        e = p - OGZ_PIECE0
        ogz = (4 * CONV_WIDTH + 2 * QK_DIM + V_DIM) // W_KBLK + 2 * (e // 4) + (e // 2) % 2 \
            + (V_DIM // W_KBLK) * (e % 2)
        m = p - GAB_PIECE0
        pm = jnp.where(p >= GAB_PIECE0, GAB_PIECE0 + m // 2 + (D_MODEL // W_KBLK) * (m % 2), p)
        plain = pm + jnp.where(pm >= k_row0 // W_KBLK, QK_DIM // W_KBLK, 0)
        piece = jnp.where(p < 4 * blk, conv, plain)
        piece = jnp.where(jnp.logical_and(p >= OGZ_PIECE0, p < OGZ_PIECE0 + 2 * V_DIM // W_KBLK),
                          ogz, piece)
        r = piece * W_KBLK
        return pl.multiple_of(r + jnp.where(r >= gate_row0, 2 * N_HEADS, 0), SUBLANES)

    for s in range(depth):
        copy(wt_hbm, src_row(s), s).start(priority=s % 2)

    def group(g, carry):
        for s in range(depth):
            p = g * depth + s
            copy(wt_hbm, src_row(p), s).wait()
            blk = slots[s][...]
            is_q = jnp.logical_and(p * W_KBLK >= 4 * CONV_WIDTH, p * W_KBLK < k_row0)
            blk = blk * jnp.where(is_q, DK ** -0.5, 1.0).astype(F32)
            blk = blk.astype(BF16)
            col0 = pl.multiple_of(p * W_KBLK, W_KBLK)
            for j in range(D_MODEL // W_KBLK):
                blk_t = lax.dot_general(eye, blk[:, j * W_KBLK:(j + 1) * W_KBLK],
                                        (((1,), (1,)), ((), ())), preferred_element_type=F32)
                win_ref[j * W_KBLK:(j + 1) * W_KBLK, pl.ds(col0, W_KBLK)] = blk_t.astype(BF16)

            @pl.when(p + depth < n_pieces)
            def _():
                copy(wt_hbm, src_row(p + depth), s).start(priority=s % 2)
        return carry

    lax.fori_loop(0, n_pieces // depth, group, 0)

    plain = [(wt_hbm, k_row0 + h * DK + r, wkt_ref, ((h - 1) % N_HEADS) * DK + r)
             for h in range(N_HEADS) for r in range(0, DK, W_KBLK)]
    for src_hbm, dst_ref in ((waout_hbm, waout_ref), (wbout_hbm, wbout_ref), (wo_hbm, wo_ref),
                             (wpg_hbm, wpg_ref), (wple_hbm, wple_ref)):
        assert dst_ref.shape[1] == D_MODEL and dst_ref.shape[0] % W_KBLK == 0
        plain += [(src_hbm, r, dst_ref, r) for r in range(0, dst_ref.shape[0], W_KBLK)]
    for i in range(min(depth, len(plain))):
        copy(plain[i][0], plain[i][1], i % depth).start(priority=i % 2)
    for i, (src_hbm, row, dst_ref, dst_row) in enumerate(plain):
        copy(src_hbm, row, i % depth).wait()
        dst_ref[dst_row:dst_row + W_KBLK, :] = slots[i % depth][...].astype(BF16)
        if i + depth < len(plain):
            nxt = plain[i + depth]
            copy(nxt[0], nxt[1], i % depth).start(priority=i % 2)

    gcopy = pltpu.make_async_copy(wt_hbm.at[pl.ds(gate_row0, 2 * N_HEADS), :], gstage_ref, sem.at[0])
    gcopy.start()
    gcopy.wait()
    gpad = jnp.concatenate([gstage_ref[...], jnp.zeros((GATE_ROWS - 2 * N_HEADS, D_MODEL), F32)], axis=0)
    wkt_ref[QK_DIM:QK_DIM + GATE_ROWS, :] = gpad.astype(BF16)


def _block_kernel(tiles_per_row,
                  x_ref, xp_ref, pp_ref, wt_hbm, waout_hbm, wbout_hbm, wo_hbm, wpg_hbm, wple_hbm,
                  gmix_ref, bgates_ref, convw_ref, convb_ref, ghead_ref, gple_ref, gfinal_ref,
                  out_ref,
                  c_ref, n_ref, m_ref, u_ref, hn_ref, hnp_ref, ya_ref, yb_ref,
                  win_ref, wkt_ref, waout_ref, wbout_ref, wo_ref, wpg_ref, wple_ref,
                  stage_ref, gstage_ref, dma_sem):
    step = pl.program_id(0)

    o = 0
    wconv_ref = win_ref.at[:, o:o + 4 * CONV_WIDTH]; o += 4 * CONV_WIDTH
    wq_ref = win_ref.at[:, o:o + QK_DIM]; o += QK_DIM
    wv_ref = win_ref.at[:, o:o + V_DIM]; o += V_DIM
    wogz_ref = win_ref.at[:, o:o + 2 * V_DIM]; o += 2 * V_DIM
    wgab_ref = win_ref.at[:, o:o + 2 * D_MODEL]

    @pl.when(step == 0)
    def _():
        _load_weights(wt_hbm, waout_hbm, wbout_hbm, wo_hbm, wpg_hbm, wple_hbm,
                      win_ref, wkt_ref, waout_ref, wbout_ref, wo_ref, wpg_ref, wple_ref,
                      [stage_ref.at[0], stage_ref.at[1], ya_ref, yb_ref], gstage_ref, dma_sem)
        for ref in (ya_ref, yb_ref, hnp_ref):
            ref[...] = jnp.zeros_like(ref)

    @pl.when(lax.rem(step, tiles_per_row) == 0)
    def _():
        c_ref[...] = jnp.zeros_like(c_ref)
        n_ref[...] = jnp.zeros_like(n_ref)
        m_ref[...] = jnp.full_like(m_ref, -jnp.inf)
        u_ref[0:SUBLANES, :] = jnp.zeros((SUBLANES, CONV_WIDTH), F32)

    def hn():
        return hn_ref[...]

    def pair_blocks(hn_w):
        n = hn_w.shape[1] // (2 * W_KBLK)
        a = [hn_w[:, (2 * j) * W_KBLK:(2 * j + 1) * W_KBLK] for j in range(n)]
        b = [hn_w[:, (2 * j + 1) * W_KBLK:(2 * j + 2) * W_KBLK] for j in range(n)]
        return a, b

    def conv_in(cb):
        c0 = cb * CONV_BLK
        cs = slice(c0, c0 + CONV_BLK)
        xa, ca = pair_blocks(_dot(hn(), wconv_ref[:, 4 * c0:4 * c0 + 2 * CONV_BLK]))
        u_ref[SUBLANES:SUBLANES + TILE, cs] = jnp.concatenate([x * c for x, c in zip(xa, ca)], axis=1)
        u0 = u_ref[SUBLANES:SUBLANES + TILE, cs]
        u1 = u_ref[SUBLANES - 1:SUBLANES - 1 + TILE, cs]
        u2 = u_ref[SUBLANES - 2:SUBLANES - 2 + TILE, cs]
        conv = (u2 * convw_ref[0:1, cs] + u1 * convw_ref[1:2, cs] + u0 * convw_ref[2:3, cs]
                + convb_ref[:, cs])
        u_ref[0:SUBLANES, cs] = u_ref[TILE:TILE + SUBLANES, cs]
        return conv

    def conv_gate(cb, conv):
        c0 = cb * CONV_BLK
        ba, za = pair_blocks(_dot(hn(), wconv_ref[:, 4 * c0 + 2 * CONV_BLK:4 * c0 + 4 * CONV_BLK]))
        gated = [b * _silu(z) for b, z in zip(ba, za)]
        return (jnp.concatenate(gated, axis=1) * conv).astype(BF16)

    def conv_out(ya_blks):
        ya_ref[...] = _dot(jnp.concatenate(ya_blks, axis=1), waout_ref[...])

    def causal_mask():
        return (lax.broadcasted_iota(jnp.int32, (TILE, TILE), 0)
                >= lax.broadcasted_iota(jnp.int32, (TILE, TILE), 1))

    def proj_kt(h):
        r0 = ((h - 1) % N_HEADS) * DK
        rows = DK + GATE_ROWS if h == 0 else DK
        res = lax.dot_general(wkt_ref[r0:r0 + rows, :], hn(), (((1,), (1,)), ((), ())),
                              preferred_element_type=F32)
        return (res[0:DK], res[DK:DK + 2 * N_HEADS]) if h == 0 else res

    def scan_lanes(v, combine, identity):
        lane = lax.broadcasted_iota(jnp.int32, v.shape, 1)
        shift = 1
        while shift < v.shape[1]:
            v = combine(v, jnp.where(lane >= shift, pltpu.roll(v, shift, axis=1), identity))
            shift *= 2
        return v

    def gate_logs(graw):
        gates = graw + bgates_ref[:, 0:1]
        lf = jnp.minimum(gates, 0.0) - jnp.log1p(jnp.exp(-jnp.abs(gates)))
        return gates[0:N_HEADS], scan_lanes(lf[N_HEADS:2 * N_HEADS], jnp.add, 0.0)

    def cummax_lanes(v):
        return scan_lanes(v, jnp.maximum, -jnp.inf)

    def chunk_terms(li, b):
        m_st = jnp.concatenate([m_ref[h, 0:1, 0:1] for h in range(N_HEADS)], axis=0)
        d = li - b
        m_row = b + jnp.maximum(m_st, cummax_lanes(d))
        inter = jnp.exp(b + m_st - m_row)
        floor = jnp.exp(-m_row)
        b_last = b[:, TILE - 1:TILE]
        g = b_last - b + li
        m_new = jnp.maximum(b_last + m_st, jnp.max(g, axis=1, keepdims=True))
        w = jnp.exp(g - m_new)
        decay = jnp.exp(b_last + m_st - m_new)
        for h in range(N_HEADS):
            m_ref[h] = jnp.broadcast_to(m_new[h:h + 1], (SUBLANES, LANES))
        rows = jnp.concatenate([b - m_row, inter, floor,
                                jnp.zeros((LANES - 3 * N_HEADS, TILE), F32)], axis=0)
        cols = rows.T
        return [dict(d_row=d[h:h + 1], c_col=cols[:, h:h + 1],
                     inter=cols[:, N_HEADS + h:N_HEADS + h + 1],
                     floor=cols[:, 2 * N_HEADS + h:2 * N_HEADS + h + 1],
                     w_row=w[h:h + 1], decay=decay[h:h + 1]) for h in range(N_HEADS)]

    def head_dexp(dk):
        return jnp.exp(jnp.where(causal_mask(), dk["c_col"] + dk["d_row"], -jnp.inf))

    def head_proj(h, kt=None):
        q = _dot(hn(), wq_ref[:, h * DK:(h + 1) * DK])
        if kt is None:
            kt = proj_kt(h)
        v = _dot(hn(), wv_ref[:, h * DV:(h + 1) * DV]).astype(BF16)
        return q, kt, v

    def head_scores(q, kt):
        return _dot(q.astype(BF16), kt.astype(BF16))

    def head_num(h, q, v, s, dk):
        p32 = s * head_dexp(dk)
        qs = (q * dk["inter"]).astype(BF16)
        num = _dot(jnp.concatenate([p32.astype(BF16), qs], axis=1),
                   jnp.concatenate([v, c_ref[h].astype(BF16)], axis=0))
        n_row = n_ref[h, 0:1, :]
        den = (jnp.sum(p32, axis=1, keepdims=True)
               + dk["inter"] * jnp.sum(q * n_row, axis=1, keepdims=True))
        return num / jnp.maximum(jnp.abs(den), dk["floor"])

    def head_ktw(kt, dk):
        ktw = kt * dk["w_row"]
        return ktw.astype(BF16), jnp.sum(ktw, axis=1, keepdims=True)

    def normaliser_upd(n_cols, dks):
        lane = lax.broadcasted_iota(jnp.int32, (DK, LANES), 1)
        cols = jnp.zeros((DK, LANES), F32)
        for h, n_col in enumerate(n_cols):
            cols = jnp.where(lane == h, n_col, cols)
        rows = cols.T
        for h in range(N_HEADS):
            n_ref[h] = dks[h]["decay"] * n_ref[h] + rows[h:h + 1]

    def head_upd(h, ktw, v, dk):
        c_ref[h] = dk["decay"] * c_ref[h] + _dot(ktw, v)

    def head_gate(h):
        og, zb = pair_blocks(_dot(hn(), wogz_ref[:, 2 * h * DV:2 * (h + 1) * DV]))
        return jnp.concatenate([_sigmoid(o) * _silu(z) for o, z in zip(og, zb)], axis=1)

    def head_y(h, hh, gate):
        return (_rms(hh, ghead_ref[:, h * DV:(h + 1) * DV]) * gate).astype(BF16)

    def heads_out(yb_blks):
        yb_ref[...] = _dot(jnp.concatenate(yb_blks, axis=1), wbout_ref[...])

    def tail_gates():
        return pair_blocks(_dot(hnp_ref[...], wgab_ref[...]))

    def tail_merge(ga, gb):
        merged = [_sigmoid(a) * ya_ref[:, j * W_KBLK:(j + 1) * W_KBLK]
                  + _sigmoid(b) * yb_ref[:, j * W_KBLK:(j + 1) * W_KBLK]
                  for j, (a, b) in enumerate(zip(ga, gb))]
        return jnp.concatenate(merged, axis=1).astype(BF16)

    def tail_out_proj(merged):
        x1 = xp_ref[...] + _dot(merged, wo_ref[...])
        return x1, _rms(x1, gple_ref[...]).astype(BF16)

    def tail_finish(x1, r1):
        pe = _dot(pp_ref[...].astype(BF16), wple_ref[...])
        ple_gate = _sigmoid(_dot(r1, wpg_ref[...]))
        out_ref[...] = _rms(x1 + ple_gate * pe, gfinal_ref[...])

    @pl.when(step == pl.num_programs(0) - 1)
    def _():
        tail_finish(*tail_out_proj(tail_merge(*tail_gates())))

    @pl.when(step < pl.num_programs(0) - 1)
    def _():
        ga, gb = tail_gates()
        hn_ref[...] = _rms(x_ref[...], gmix_ref[...]).astype(BF16)
        conv0 = conv_in(0)
        kt0, graw = proj_kt(0)
        x1, r1 = tail_out_proj(tail_merge(ga, gb))
        ya0 = conv_gate(0, conv0)
        dks = chunk_terms(*gate_logs(graw))
        conv1 = conv_in(1)
        tail_finish(x1, r1)
        ya1 = conv_gate(1, conv1)

        q0, kt0, v0 = head_proj(0, kt0)
        conv_out([ya0, ya1])
        s0 = head_scores(q0, kt0)
        q1, kt1, v1 = head_proj(1)
        hh0 = head_num(0, q0, v0, s0, dks[0])
        ktw0, ncol0 = head_ktw(kt0, dks[0])
        s1 = head_scores(q1, kt1)
        gate0 = head_gate(0)
        q2, kt2, v2 = head_proj(2)
        yb0 = head_y(0, hh0, gate0)
        hh1 = head_num(1, q1, v1, s1, dks[1])
        ktw1, ncol1 = head_ktw(kt1, dks[1])
        s2 = head_scores(q2, kt2)
        gate1 = head_gate(1)
        q3, kt3, v3 = head_proj(3)
        yb1 = head_y(1, hh1, gate1)
        hh2 = head_num(2, q2, v2, s2, dks[2])
        ktw2, ncol2 = head_ktw(kt2, dks[2])
        s3 = head_scores(q3, kt3)
        gate2 = head_gate(2)
        yb2 = head_y(2, hh2, gate2)
        hh3 = head_num(3, q3, v3, s3, dks[3])
        ktw3, ncol3 = head_ktw(kt3, dks[3])
        gate3 = head_gate(3)
        head_upd(0, ktw0, v0, dks[0])
        head_upd(1, ktw1, v1, dks[1])
        head_upd(2, ktw2, v2, dks[2])
        heads_out([yb0, yb1, yb2, head_y(3, hh3, gate3)])
        head_upd(3, ktw3, v3, dks[3])
        normaliser_upd([ncol0, ncol1, ncol2, ncol3], dks)
        hnp_ref[...] = hn()


def _const_spec(shape):
    return pl.BlockSpec(shape, lambda s: (0,) * len(shape), pipeline_mode=pl.Buffered(1))


def kernel(x, p, g_mix, w_in, conv_w, conv_b, w_a_out, b_gates, g_head, w_b_out, w_o, g_ple,
           w_ple_gate, w_ple, g_final):
    bsz, seq, _ = x.shape
    assert w_in.shape[0] == 1 and seq % TILE == 0 and conv_w.shape[1] == CONV_K == 3
    tiles_per_row = seq // TILE
    n_tiles = bsz * tiles_per_row
    assert w_in.shape[2] == 4 * CONV_WIDTH + 2 * QK_DIM + 3 * V_DIM + 2 * N_HEADS + 2 * D_MODEL
    wt = jnp.transpose(w_in[0])
    b_g = jnp.broadcast_to(b_gates[0].reshape(2 * N_HEADS, 1), (2 * N_HEADS, LANES))

    hbm_weights = (wt, w_a_out[0], w_b_out[0], w_o[0], w_ple_gate[0], w_ple[0])
    consts = (
        g_mix[0].reshape(1, D_MODEL), b_g, conv_w[0], conv_b[0].reshape(1, CONV_WIDTH),
        g_head[0].reshape(1, V_DIM), g_ple[0].reshape(1, D_MODEL), g_final.reshape(1, D_MODEL),
    )

    def cur_map(s):
        c = jnp.minimum(s, n_tiles - 1)
        return (c // tiles_per_row, c % tiles_per_row, 0)

    def prev_map(s):
        c = jnp.maximum(s - 1, 0)
        return (c // tiles_per_row, c % tiles_per_row, 0)

    in_specs = [
        pl.BlockSpec((None, TILE, D_MODEL), cur_map),
        pl.BlockSpec((None, TILE, D_MODEL), prev_map),
        pl.BlockSpec((None, TILE, PLE_DIM), prev_map),
    ] + [pl.BlockSpec(memory_space=pl.ANY)] * len(hbm_weights) + [_const_spec(a.shape) for a in consts]

    return pl.pallas_call(
        functools.partial(_block_kernel, tiles_per_row),
        grid=(n_tiles + 1,),
        in_specs=in_specs,
        out_specs=pl.BlockSpec((None, TILE, D_MODEL), prev_map),
        out_shape=jax.ShapeDtypeStruct(x.shape, x.dtype),
        scratch_shapes=[
            pltpu.VMEM((N_HEADS, DK, DV), F32),
            pltpu.VMEM((N_HEADS, SUBLANES, DK), F32),
            pltpu.VMEM((N_HEADS, SUBLANES, LANES), F32),
            pltpu.VMEM((TILE + SUBLANES, CONV_WIDTH), F32),
            pltpu.VMEM((TILE, D_MODEL), BF16),
            pltpu.VMEM((TILE, D_MODEL), BF16),
            pltpu.VMEM((TILE, D_MODEL), F32),
            pltpu.VMEM((TILE, D_MODEL), F32),
            pltpu.VMEM((D_MODEL, 4 * CONV_WIDTH + QK_DIM + 3 * V_DIM + 2 * D_MODEL), BF16),
            pltpu.VMEM((QK_DIM + GATE_ROWS, D_MODEL), BF16),
            pltpu.VMEM((CONV_WIDTH, D_MODEL), BF16),
            pltpu.VMEM((V_DIM, D_MODEL), BF16),
            pltpu.VMEM((D_MODEL, D_MODEL), BF16),
            pltpu.VMEM((D_MODEL, D_MODEL), BF16),
            pltpu.VMEM((PLE_DIM, D_MODEL), BF16),
            pltpu.VMEM((2, W_KBLK, D_MODEL), F32),
            pltpu.VMEM((2 * N_HEADS, D_MODEL), F32),
            pltpu.SemaphoreType.DMA((4,)),
        ],
        compiler_params=pltpu.CompilerParams(
            dimension_semantics=("arbitrary",),
            vmem_limit_bytes=VMEM_LIMIT_BYTES,
        ),
        name="hybrid_block",
    )(x, x, p[0], *hbm_weights, *consts)
```

```python
import functools

import jax
import jax.numpy as jnp
from jax import lax
from jax.experimental import pallas as pl
from jax.experimental.pallas import tpu as pltpu

D_MODEL = 1024
PLE_DIM = 256
CONV_WIDTH = 1024
CONV_K = 3
N_HEADS = 4
QK_DIM = 1024
V_DIM = 2048
DK = QK_DIM // N_HEADS
DV = V_DIM // N_HEADS
EPS = 1e-6

LANES = 128
SUBLANES = 8
TILE = 256
CONV_BLK = 512
W_KBLK = 256
OGZ_PIECE0 = (4 * CONV_WIDTH + QK_DIM + V_DIM) // W_KBLK
GAB_PIECE0 = OGZ_PIECE0 + 2 * V_DIM // W_KBLK
GATE_ROWS = 16
V7X_VMEM_BYTES = 64 * 1024 * 1024
VMEM_LIMIT_BYTES = V7X_VMEM_BYTES - 3 * 1024 * 1024

BF16 = jnp.bfloat16
F32 = jnp.float32


def _dot(a, b):
    return jnp.dot(a, b, preferred_element_type=F32)


def _sigmoid(v):
    return 0.5 * jnp.tanh(0.5 * v) + 0.5


def _silu(v):
    return v * _sigmoid(v)


def _rms(v, g):
    return v * lax.rsqrt(jnp.mean(v * v, axis=-1, keepdims=True) + EPS) * g


def _load_weights(wt_hbm, waout_hbm, wbout_hbm, wo_hbm, wpg_hbm, wple_hbm,
                  win_ref, wkt_ref, waout_ref, wbout_ref, wo_ref, wpg_ref, wple_ref,
                  slots, gstage_ref, sem):
    depth = len(slots)
    eye = (lax.broadcasted_iota(jnp.int32, (W_KBLK, W_KBLK), 0)
           == lax.broadcasted_iota(jnp.int32, (W_KBLK, W_KBLK), 1)).astype(BF16)

    def copy(src_hbm, row, s):
        return pltpu.make_async_copy(src_hbm.at[pl.ds(row, W_KBLK), :], slots[s], sem.at[s])

    k_row0 = 4 * CONV_WIDTH + QK_DIM
    gate_row0 = k_row0 + QK_DIM + 3 * V_DIM
    n_pieces = win_ref.shape[1] // W_KBLK
    assert n_pieces % depth == 0

    def src_row(p):
        blk = CONV_WIDTH // W_KBLK
        c = (p // 8) * 2 + (p // 2) % 2
        conv = c + blk * ((p // 4) % 2) + 2 * blk * (p % 2)
        e = p - OGZ_PIECE0
        ogz = (4 * CONV_WIDTH + 2 * QK_DIM + V_DIM) // W_KBLK + 2 * (e // 4) + (e // 2) % 2 \
            + (V_DIM // W_KBLK) * (e % 2)
        m = p - GAB_PIECE0
        pm = jnp.where(p >= GAB_PIECE0, GAB_PIECE0 + m // 2 + (D_MODEL // W_KBLK) * (m % 2), p)
        plain = pm + jnp.where(pm >= k_row0 // W_KBLK, QK_DIM // W_KBLK, 0)
        piece = jnp.where(p < 4 * blk, conv, plain)
        piece = jnp.where(jnp.logical_and(p >= OGZ_PIECE0, p < OGZ_PIECE0 + 2 * V_DIM // W_KBLK),
                          ogz, piece)
        r = piece * W_KBLK
        return pl.multiple_of(r + jnp.where(r >= gate_row0, 2 * N_HEADS, 0), SUBLANES)

    for s in range(depth):
        copy(wt_hbm, src_row(s), s).start(priority=s % 2)

    def group(g, carry):
        for s in range(depth):
            p = g * depth + s
            copy(wt_hbm, src_row(p), s).wait()
            blk = slots[s][...]
            is_q = jnp.logical_and(p * W_KBLK >= 4 * CONV_WIDTH, p * W_KBLK < k_row0)
            blk = blk * jnp.where(is_q, DK ** -0.5, 1.0).astype(F32)
            blk = blk.astype(BF16)
            col0 = pl.multiple_of(p * W_KBLK, W_KBLK)
            for j in range(D_MODEL // W_KBLK):
                blk_t = lax.dot_general(eye, blk[:, j * W_KBLK:(j + 1) * W_KBLK],
                                        (((1,), (1,)), ((), ())), preferred_element_type=F32)
                win_ref[j * W_KBLK:(j + 1) * W_KBLK, pl.ds(col0, W_KBLK)] = blk_t.astype(BF16)

            @pl.when(p + depth < n_pieces)
            def _():
                copy(wt_hbm, src_row(p + depth), s).start(priority=s % 2)
        return carry

    lax.fori_loop(0, n_pieces // depth, group, 0)

    plain = [(wt_hbm, k_row0 + h * DK + r, wkt_ref, ((h - 1) % N_HEADS) * DK + r)
             for h in range(N_HEADS) for r in range(0, DK, W_KBLK)]
    for src_hbm, dst_ref in ((waout_hbm, waout_ref), (wbout_hbm, wbout_ref), (wo_hbm, wo_ref),
                             (wpg_hbm, wpg_ref), (wple_hbm, wple_ref)):
        assert dst_ref.shape[1] == D_MODEL and dst_ref.shape[0] % W_KBLK == 0
        plain += [(src_hbm, r, dst_ref, r) for r in range(0, dst_ref.shape[0], W_KBLK)]
    for i in range(min(depth, len(plain))):
        copy(plain[i][0], plain[i][1], i % depth).start(priority=i % 2)
    for i, (src_hbm, row, dst_ref, dst_row) in enumerate(plain):
        copy(src_hbm, row, i % depth).wait()
        dst_ref[dst_row:dst_row + W_KBLK, :] = slots[i % depth][...].astype(BF16)
        if i + depth < len(plain):
            nxt = plain[i + depth]
            copy(nxt[0], nxt[1], i % depth).start(priority=i % 2)

    gcopy = pltpu.make_async_copy(wt_hbm.at[pl.ds(gate_row0, 2 * N_HEADS), :], gstage_ref, sem.at[0])
    gcopy.start()
    gcopy.wait()
    gpad = jnp.concatenate([gstage_ref[...], jnp.zeros((GATE_ROWS - 2 * N_HEADS, D_MODEL), F32)], axis=0)
    wkt_ref[QK_DIM:QK_DIM + GATE_ROWS, :] = gpad.astype(BF16)


def _block_kernel(tiles_per_row,
                  x_ref, xp_ref, pp_ref, wt_hbm, waout_hbm, wbout_hbm, wo_hbm, wpg_hbm, wple_hbm,
                  gmix_ref, bgates_ref, convw_ref, convb_ref, ghead_ref, gple_ref, gfinal_ref,
                  out_ref,
                  c_ref, n_ref, m_ref, u_ref, hn_ref, hnp_ref, ya_ref, yb_ref,
                  win_ref, wkt_ref, waout_ref, wbout_ref, wo_ref, wpg_ref, wple_ref,
                  stage_ref, gstage_ref, dma_sem):
    step = pl.program_id(0)

    o = 0
    wconv_ref = win_ref.at[:, o:o + 4 * CONV_WIDTH]; o += 4 * CONV_WIDTH
    wq_ref = win_ref.at[:, o:o + QK_DIM]; o += QK_DIM
    wv_ref = win_ref.at[:, o:o + V_DIM]; o += V_DIM
    wogz_ref = win_ref.at[:, o:o + 2 * V_DIM]; o += 2 * V_DIM
    wgab_ref = win_ref.at[:, o:o + 2 * D_MODEL]

    @pl.when(step == 0)
    def _():
        _load_weights(wt_hbm, waout_hbm, wbout_hbm, wo_hbm, wpg_hbm, wple_hbm,
                      win_ref, wkt_ref, waout_ref, wbout_ref, wo_ref, wpg_ref, wple_ref,
                      [stage_ref.at[0], stage_ref.at[1], ya_ref, yb_ref], gstage_ref, dma_sem)
        for ref in (ya_ref, yb_ref, hnp_ref):
            ref[...] = jnp.zeros_like(ref)

    @pl.when(lax.rem(step, tiles_per_row) == 0)
    def _():
        c_ref[...] = jnp.zeros_like(c_ref)
        n_ref[...] = jnp.zeros_like(n_ref)
        m_ref[...] = jnp.full_like(m_ref, -jnp.inf)
        u_ref[0:SUBLANES, :] = jnp.zeros((SUBLANES, CONV_WIDTH), F32)

    def hn():
        return hn_ref[...]

    def pair_blocks(hn_w):
        n = hn_w.shape[1] // (2 * W_KBLK)
        a = [hn_w[:, (2 * j) * W_KBLK:(2 * j + 1) * W_KBLK] for j in range(n)]
        b = [hn_w[:, (2 * j + 1) * W_KBLK:(2 * j + 2) * W_KBLK] for j in range(n)]
        return a, b

    def conv_in(cb):
        c0 = cb * CONV_BLK
        cs = slice(c0, c0 + CONV_BLK)
        xa, ca = pair_blocks(_dot(hn(), wconv_ref[:, 4 * c0:4 * c0 + 2 * CONV_BLK]))
        u_ref[SUBLANES:SUBLANES + TILE, cs] = jnp.concatenate([x * c for x, c in zip(xa, ca)], axis=1)
        u0 = u_ref[SUBLANES:SUBLANES + TILE, cs]
        u1 = u_ref[SUBLANES - 1:SUBLANES - 1 + TILE, cs]
        u2 = u_ref[SUBLANES - 2:SUBLANES - 2 + TILE, cs]
        conv = (u2 * convw_ref[0:1, cs] + u1 * convw_ref[1:2, cs] + u0 * convw_ref[2:3, cs]
                + convb_ref[:, cs])
        u_ref[0:SUBLANES, cs] = u_ref[TILE:TILE + SUBLANES, cs]
        return conv

    def conv_gate(cb, conv):
        c0 = cb * CONV_BLK
        ba, za = pair_blocks(_dot(hn(), wconv_ref[:, 4 * c0 + 2 * CONV_BLK:4 * c0 + 4 * CONV_BLK]))
        gated = [b * _silu(z) for b, z in zip(ba, za)]
        return (jnp.concatenate(gated, axis=1) * conv).astype(BF16)

    def conv_out(ya_blks):
        ya_ref[...] = _dot(jnp.concatenate(ya_blks, axis=1), waout_ref[...])

    def causal_mask():
        return (lax.broadcasted_iota(jnp.int32, (TILE, TILE), 0)
                >= lax.broadcasted_iota(jnp.int32, (TILE, TILE), 1))

    def proj_kt(h):
        r0 = ((h - 1) % N_HEADS) * DK
        rows = DK + GATE_ROWS if h == 0 else DK
        res = lax.dot_general(wkt_ref[r0:r0 + rows, :], hn(), (((1,), (1,)), ((), ())),
                              preferred_element_type=F32)
        return (res[0:DK], res[DK:DK + 2 * N_HEADS]) if h == 0 else res

    def gate_logs(graw):
        gates = graw + bgates_ref[:, 0:1]
        lf = jnp.minimum(gates, 0.0) - jnp.log1p(jnp.exp(-jnp.abs(gates)))
        lf_hi = lf.astype(BF16)
        lf_lo = (lf - lf_hi.astype(F32)).astype(BF16)
        triu = (lax.broadcasted_iota(jnp.int32, (TILE, TILE), 0)
                <= lax.broadcasted_iota(jnp.int32, (TILE, TILE), 1)).astype(BF16)
        bcum = _dot(jnp.concatenate([lf_hi, lf_lo], axis=1), jnp.concatenate([triu, triu], axis=0))
        return gates[0:N_HEADS], bcum[N_HEADS:2 * N_HEADS]

    def cummax_lanes(v):
        half = jnp.concatenate([v[:, 0:LANES], v[:, LANES:2 * LANES]], axis=0)
        lane = lax.broadcasted_iota(jnp.int32, half.shape, 1)
        shift = 1
        while shift < LANES:
            half = jnp.maximum(half, jnp.where(lane >= shift, pltpu.roll(half, shift, axis=1), -jnp.inf))
            shift *= 2
        first, second = half[0:N_HEADS], half[N_HEADS:2 * N_HEADS]
        return jnp.concatenate([first, jnp.maximum(second, first[:, LANES - 1:LANES])], axis=1)

    def chunk_terms(li, b):
        m_st = jnp.concatenate([m_ref[h, 0:1, 0:1] for h in range(N_HEADS)], axis=0)
        d = li - b
        m_row = b + jnp.maximum(m_st, cummax_lanes(d))
        inter = jnp.exp(b + m_st - m_row)
        floor = jnp.exp(-m_row)
        b_last = b[:, TILE - 1:TILE]
        g = b_last - b + li
        m_new = jnp.maximum(b_last + m_st, jnp.max(g, axis=1, keepdims=True))
        w = jnp.exp(g - m_new)
        decay = jnp.exp(b_last + m_st - m_new)
        for h in range(N_HEADS):
            m_ref[h] = jnp.broadcast_to(m_new[h:h + 1], (SUBLANES, LANES))
        rows = jnp.concatenate([b - m_row, inter, floor,
                                jnp.zeros((LANES - 3 * N_HEADS, TILE), F32)], axis=0)
        cols = rows.T
        return [dict(d_row=d[h:h + 1], c_col=cols[:, h:h + 1],
                     inter=cols[:, N_HEADS + h:N_HEADS + h + 1],
                     floor=cols[:, 2 * N_HEADS + h:2 * N_HEADS + h + 1],
                     w_row=w[h:h + 1], decay=decay[h:h + 1]) for h in range(N_HEADS)]

    def head_dexp(dk):
        return jnp.exp(jnp.where(causal_mask(), dk["c_col"] + dk["d_row"], -jnp.inf))

    def head_proj(h, kt=None):
        q = _dot(hn(), wq_ref[:, h * DK:(h + 1) * DK])
        if kt is None:
            kt = proj_kt(h)
        v = _dot(hn(), wv_ref[:, h * DV:(h + 1) * DV]).astype(BF16)
        return q, kt, v

    def head_scores(q, kt):
        return _dot(q.astype(BF16), kt.astype(BF16))

    def head_num(h, q, v, s, dk):
        p32 = s * head_dexp(dk)
        qs = (q * dk["inter"]).astype(BF16)
        num = _dot(jnp.concatenate([p32.astype(BF16), qs], axis=1),
                   jnp.concatenate([v, c_ref[h].astype(BF16)], axis=0))
        n_row = n_ref[h, 0:1, :]
        den = (jnp.sum(p32, axis=1, keepdims=True)
               + dk["inter"] * jnp.sum(q * n_row, axis=1, keepdims=True))
        return num / jnp.maximum(jnp.abs(den), dk["floor"])

    def head_ktw(kt, dk):
        ktw = kt * dk["w_row"]
        return ktw.astype(BF16), jnp.sum(ktw, axis=1, keepdims=True)

    def normaliser_upd(n_cols, dks):
        lane = lax.broadcasted_iota(jnp.int32, (DK, LANES), 1)
        cols = jnp.zeros((DK, LANES), F32)
        for h, n_col in enumerate(n_cols):
            cols = jnp.where(lane == h, n_col, cols)
        rows = cols.T
        for h in range(N_HEADS):
            n_ref[h] = dks[h]["decay"] * n_ref[h] + rows[h:h + 1]

    def head_upd(h, ktw, v, dk):
        c_ref[h] = dk["decay"] * c_ref[h] + _dot(ktw, v)

    def head_gate(h):
        og, zb = pair_blocks(_dot(hn(), wogz_ref[:, 2 * h * DV:2 * (h + 1) * DV]))
        return jnp.concatenate([_sigmoid(o) * _silu(z) for o, z in zip(og, zb)], axis=1)

    def head_y(h, hh, gate):
        return (_rms(hh, ghead_ref[:, h * DV:(h + 1) * DV]) * gate).astype(BF16)

    def heads_out(yb_blks):
        yb_ref[...] = _dot(jnp.concatenate(yb_blks, axis=1), wbout_ref[...])

    def tail_gates():
        return pair_blocks(_dot(hnp_ref[...], wgab_ref[...]))

    def tail_merge(ga, gb):
        merged = [_sigmoid(a) * ya_ref[:, j * W_KBLK:(j + 1) * W_KBLK]
                  + _sigmoid(b) * yb_ref[:, j * W_KBLK:(j + 1) * W_KBLK]
                  for j, (a, b) in enumerate(zip(ga, gb))]
        return jnp.concatenate(merged, axis=1).astype(BF16)

    def tail_out_proj(merged):
        x1 = xp_ref[...] + _dot(merged, wo_ref[...])
        return x1, _rms(x1, gple_ref[...]).astype(BF16)

    def tail_finish(x1, r1):
        pe = _dot(pp_ref[...].astype(BF16), wple_ref[...])
        ple_gate = _sigmoid(_dot(r1, wpg_ref[...]))
        out_ref[...] = _rms(x1 + ple_gate * pe, gfinal_ref[...])

    @pl.when(step == pl.num_programs(0) - 1)
    def _():
        tail_finish(*tail_out_proj(tail_merge(*tail_gates())))

    @pl.when(step < pl.num_programs(0) - 1)
    def _():
        ga, gb = tail_gates()
        hn_ref[...] = _rms(x_ref[...], gmix_ref[...]).astype(BF16)
        kt0, graw = proj_kt(0)
        conv0 = conv_in(0)
        x1, r1 = tail_out_proj(tail_merge(ga, gb))
        dks = chunk_terms(*gate_logs(graw))
        ya0 = conv_gate(0, conv0)
        conv1 = conv_in(1)
        tail_finish(x1, r1)
        ya1 = conv_gate(1, conv1)

        q0, kt0, v0 = head_proj(0, kt0)
        conv_out([ya0, ya1])
        s0 = head_scores(q0, kt0)
        q1, kt1, v1 = head_proj(1)
        hh0 = head_num(0, q0, v0, s0, dks[0])
        ktw0, ncol0 = head_ktw(kt0, dks[0])
        s1 = head_scores(q1, kt1)
        gate0 = head_gate(0)
        q2, kt2, v2 = head_proj(2)
        yb0 = head_y(0, hh0, gate0)
        hh1 = head_num(1, q1, v1, s1, dks[1])
        ktw1, ncol1 = head_ktw(kt1, dks[1])
        s2 = head_scores(q2, kt2)
        gate1 = head_gate(1)
        q3, kt3, v3 = head_proj(3)
        yb1 = head_y(1, hh1, gate1)
        hh2 = head_num(2, q2, v2, s2, dks[2])
        ktw2, ncol2 = head_ktw(kt2, dks[2])
        s3 = head_scores(q3, kt3)
        gate2 = head_gate(2)
        yb2 = head_y(2, hh2, gate2)
        hh3 = head_num(3, q3, v3, s3, dks[3])
        ktw3, ncol3 = head_ktw(kt3, dks[3])
        gate3 = head_gate(3)
        head_upd(0, ktw0, v0, dks[0])
        head_upd(1, ktw1, v1, dks[1])
        head_upd(2, ktw2, v2, dks[2])
        heads_out([yb0, yb1, yb2, head_y(3, hh3, gate3)])
        head_upd(3, ktw3, v3, dks[3])
        normaliser_upd([ncol0, ncol1, ncol2, ncol3], dks)
        hnp_ref[...] = hn()


def _const_spec(shape):
    return pl.BlockSpec(shape, lambda s: (0,) * len(shape), pipeline_mode=pl.Buffered(1))


def kernel(x, p, g_mix, w_in, conv_w, conv_b, w_a_out, b_gates, g_head, w_b_out, w_o, g_ple,
           w_ple_gate, w_ple, g_final):
    bsz, seq, _ = x.shape
    assert w_in.shape[0] == 1 and seq % TILE == 0 and conv_w.shape[1] == CONV_K == 3
    assert TILE == 2 * LANES
    tiles_per_row = seq // TILE
    n_tiles = bsz * tiles_per_row
    assert w_in.shape[2] == 4 * CONV_WIDTH + 2 * QK_DIM + 3 * V_DIM + 2 * N_HEADS + 2 * D_MODEL
    wt = jnp.transpose(w_in[0])
    b_g = jnp.broadcast_to(b_gates[0].reshape(2 * N_HEADS, 1), (2 * N_HEADS, LANES))

    hbm_weights = (wt, w_a_out[0], w_b_out[0], w_o[0], w_ple_gate[0], w_ple[0])
    consts = (
        g_mix[0].reshape(1, D_MODEL), b_g, conv_w[0], conv_b[0].reshape(1, CONV_WIDTH),
        g_head[0].reshape(1, V_DIM), g_ple[0].reshape(1, D_MODEL), g_final.reshape(1, D_MODEL),
    )

    def cur_map(s):
        c = jnp.minimum(s, n_tiles - 1)
        return (c // tiles_per_row, c % tiles_per_row, 0)

    def prev_map(s):
        c = jnp.maximum(s - 1, 0)
        return (c // tiles_per_row, c % tiles_per_row, 0)

    in_specs = [
        pl.BlockSpec((None, TILE, D_MODEL), cur_map),
        pl.BlockSpec((None, TILE, D_MODEL), prev_map),
        pl.BlockSpec((None, TILE, PLE_DIM), prev_map),
    ] + [pl.BlockSpec(memory_space=pl.ANY)] * len(hbm_weights) + [_const_spec(a.shape) for a in consts]

    return pl.pallas_call(
        functools.partial(_block_kernel, tiles_per_row),
        grid=(n_tiles + 1,),
        in_specs=in_specs,
        out_specs=pl.BlockSpec((None, TILE, D_MODEL), prev_map),
        out_shape=jax.ShapeDtypeStruct(x.shape, x.dtype),
        scratch_shapes=[
            pltpu.VMEM((N_HEADS, DK, DV), F32),
            pltpu.VMEM((N_HEADS, SUBLANES, DK), F32),
            pltpu.VMEM((N_HEADS, SUBLANES, LANES), F32),
            pltpu.VMEM((TILE + SUBLANES, CONV_WIDTH), F32),
            pltpu.VMEM((TILE, D_MODEL), BF16),
            pltpu.VMEM((TILE, D_MODEL), BF16),
            pltpu.VMEM((TILE, D_MODEL), F32),
            pltpu.VMEM((TILE, D_MODEL), F32),
            pltpu.VMEM((D_MODEL, 4 * CONV_WIDTH + QK_DIM + 3 * V_DIM + 2 * D_MODEL), BF16),
            pltpu.VMEM((QK_DIM + GATE_ROWS, D_MODEL), BF16),
            pltpu.VMEM((CONV_WIDTH, D_MODEL), BF16),
            pltpu.VMEM((V_DIM, D_MODEL), BF16),
            pltpu.VMEM((D_MODEL, D_MODEL), BF16),
            pltpu.VMEM((D_MODEL, D_MODEL), BF16),
            pltpu.VMEM((PLE_DIM, D_MODEL), BF16),
            pltpu.VMEM((2, W_KBLK, D_MODEL), F32),
            pltpu.VMEM((2 * N_HEADS, D_MODEL), F32),
            pltpu.SemaphoreType.DMA((4,)),
        ],
        compiler_params=pltpu.CompilerParams(
            dimension_semantics=("arbitrary",),
            vmem_limit_bytes=VMEM_LIMIT_BYTES,
        ),
        name="hybrid_block",
    )(x, x, p[0], *hbm_weights, *consts)
```
